```python
import math, functools
import jax, jax.numpy as jnp
from jax import lax
import numpy as np

D_MODEL = 1024
BATCH = 2
SEQ = 8192
DEPTH = 1
DEC_BATCH = 128
DEC_SEQ = 1
PAST_LEN = 2048
PAGE_SIZE = 128

D_CONV = 512
CONV_WIDTH = 31
N_HEADS = 8
HEAD_DIM = 64
ATT_W = N_HEADS * HEAD_DIM
N_IDX_HEADS = 8
IDX_DIM = 32
TOPK_MAX = 256
Q_BLOCK = 128
N_MEM = 256
MEM_HEADS = 4
MEM_HEAD_DIM = 128
MEM_W = MEM_HEADS * MEM_HEAD_DIM
N_BRANCH = 3
N_BUCKETS = 32
REL_MAX_DIST = 128
N_EXPERTS = 256
TOP_K = 8
D_EXPERT = 256
D_SHARED = 256
ROUTED_SCALE = 2.5
PROMPT_MOE_BLOCK = 128
SAMPLE_MOE_BLOCK = 16
ALPHA = (2 * DEPTH) ** 0.25
BETA = (8 * DEPTH) ** -0.25
LN_EPS = 1e-5
ATT_SCALE = HEAD_DIM ** -0.5
MEM_SCALE = MEM_HEAD_DIM ** -0.5
NEG_INF = -1e30
N_IN = 2 * D_CONV + 3 * ATT_W + N_IDX_HEADS * IDX_DIM + IDX_DIM + N_IDX_HEADS + MEM_W + N_BRANCH * D_MODEL

kernel_name = 'gated_conv_dsa_memory_moe_step'


def layer_norm(x, g, b):
    xf = x.astype(jnp.float32)
    mu = jnp.mean(xf, -1, keepdims=True)
    var = jnp.mean(jnp.square(xf - mu), -1, keepdims=True)
    y = (xf - mu) * lax.rsqrt(var + LN_EPS) * g.astype(jnp.float32) + b.astype(jnp.float32)
    return y.astype(x.dtype)


def split_proj(z):
    sizes = (2 * D_CONV, ATT_W, ATT_W, ATT_W, N_IDX_HEADS * IDX_DIM, IDX_DIM, N_IDX_HEADS, MEM_W, N_BRANCH * D_MODEL)
    idx, acc = [], 0
    for s in sizes[:-1]:
        acc += s
        idx.append(acc)
    return jnp.split(z, idx, axis=-1)


def rel_bucket(dist):
    n = jnp.maximum(dist, 0)
    max_exact = N_BUCKETS // 2
    nf = jnp.maximum(n, max_exact).astype(jnp.float32)
    large = max_exact + (jnp.log(nf / max_exact) / math.log(REL_MAX_DIST / max_exact)
                         * (N_BUCKETS - max_exact)).astype(jnp.int32)
    return jnp.where(n < max_exact, n, jnp.minimum(large, N_BUCKETS - 1))


def conv_branch(glu_in, conv_buf, conv_w, conv_b, ln_g, ln_b, w_conv_out):
    a, gate = jnp.split(glu_in, 2, axis=-1)
    u = a * jax.nn.sigmoid(gate)
    full = jnp.concatenate([conv_buf, u], axis=1)
    y = lax.conv_general_dilated(full, conv_w[:, None, :], window_strides=(1,), padding='VALID',
                                 dimension_numbers=('NWC', 'WIO', 'NWC'),
                                 feature_group_count=D_CONV) + conv_b
    y = jax.nn.silu(layer_norm(y, ln_g, ln_b))
    return y @ w_conv_out, full[:, full.shape[1] - (CONV_WIDTH - 1):]


def indexer_scores(q_idx, w_idx, k_idx):
    dots = jnp.einsum('nthd,nsd->nths', q_idx, k_idx, preferred_element_type=jnp.float32)
    return jnp.einsum('nth,nths->nts', w_idx.astype(jnp.float32), jax.nn.relu(dots))


def sparse_attend(q, k_sel, v_sel, sel, qpos, rel_bias):
    dist = qpos[..., None] - sel
    logits = jnp.einsum('nthd,ntkhd->nthk', q, k_sel, preferred_element_type=jnp.float32) * ATT_SCALE
    logits = logits + jnp.moveaxis(rel_bias[rel_bucket(dist)], -1, 2).astype(jnp.float32)
    logits = jnp.where((dist >= 0)[:, :, None, :], logits, NEG_INF)
    p = jax.nn.softmax(logits, axis=-1)
    return jnp.einsum('nthk,ntkhd->nthd', p.astype(v_sel.dtype), v_sel)


def take_rows(a, i):
    return jax.vmap(lambda aa, ii: aa[ii])(a, i)


def prompt_dsa(q, k, v, q_idx, w_idx, k_idx, rel_bias):
    N, S = q.shape[:2]
    top = min(TOPK_MAX, S // 4)
    nb = S // Q_BLOCK
    kpos = jnp.arange(S, dtype=jnp.int32)

    def to_blocks(a):
        return jnp.swapaxes(a.reshape((N, nb, Q_BLOCK) + a.shape[2:]), 0, 1)

    def block(args):
        qb, qib, wib, pos = args
        scores = indexer_scores(qib, wib, k_idx)
        scores = jnp.where(kpos[None, None, :] <= pos[None, :, None], scores, -jnp.inf)
        _, sel = lax.top_k(scores, top)
        return sparse_attend(qb, take_rows(k, sel), take_rows(v, sel), sel, pos[None, :], rel_bias)

    out = lax.map(block, (to_blocks(q), to_blocks(q_idx), to_blocks(w_idx), kpos.reshape(nb, Q_BLOCK)))
    return jnp.swapaxes(out, 0, 1).reshape(q.shape)


def sample_dsa(q, k, v, q_idx, w_idx, k_idx, cache_k, cache_v, cache_kidx, page_table, rel_bias):
    N, T = q.shape[:2]
    past = page_table.shape[1] * PAGE_SIZE
    L = past + T
    top = min(TOPK_MAX, L // 4)
    kidx_past = cache_kidx[page_table].reshape(N, past, IDX_DIM)
    kidx_all = jnp.concatenate([kidx_past, k_idx], axis=1)
    qpos = past + jnp.arange(T, dtype=jnp.int32)
    kpos = jnp.arange(L, dtype=jnp.int32)
    scores = indexer_scores(q_idx, w_idx, kidx_all)
    scores = jnp.where(kpos[None, None, :] <= qpos[None, :, None], scores, -jnp.inf)
    _, sel = lax.top_k(scores, top)
    in_past = (sel < past)[..., None, None]
    sp = jnp.minimum(sel, past - 1)
    phys = take_rows(page_table, sp // PAGE_SIZE)
    off = sp % PAGE_SIZE
    sn = jnp.clip(sel - past, 0, T - 1)
    k_sel = jnp.where(in_past, cache_k[phys, off], take_rows(k, sn))
    v_sel = jnp.where(in_past, cache_v[phys, off], take_rows(v, sn))
    return sparse_attend(q, k_sel, v_sel, sel, qpos[None, :], rel_bias)


def memory_kv(mem, w_mem_kv):
    N, M, _ = mem.shape
    mk, mv = jnp.split(mem @ w_mem_kv, 2, axis=-1)
    return (mk.reshape(N, M, MEM_HEADS, MEM_HEAD_DIM), mv.reshape(N, M, MEM_HEADS, MEM_HEAD_DIM))


def mem_attend(q, mk, mv):
    N, T = q.shape[:2]
    logits = jnp.einsum('nthd,nmhd->nthm', q, mk, preferred_element_type=jnp.float32) * MEM_SCALE
    p = jax.nn.softmax(logits, axis=-1)
    return jnp.einsum('nthm,nmhd->nthd', p.astype(mv.dtype), mv).reshape(N, T, MEM_W)


def moe_ffn(x, w_router, router_bias, w_exp_gu, w_exp_down, w_sh_gu, w_sh_down, blk):
    shp = x.shape
    xt = x.reshape(-1, D_MODEL)
    n = xt.shape[0]
    scores = jax.nn.sigmoid((xt @ w_router).astype(jnp.float32))
    _, top_e = lax.top_k(scores + router_bias.astype(jnp.float32), TOP_K)
    top_s = jnp.take_along_axis(scores, top_e, axis=1)
    gates = top_s / jnp.sum(top_s, -1, keepdims=True) * ROUTED_SCALE
    nk = n * TOP_K
    e_flat = top_e.reshape(-1)
    t_flat = jnp.repeat(jnp.arange(n, dtype=jnp.int32), TOP_K)
    order = jnp.argsort(e_flat)
    e_s, t_s, g_s = e_flat[order], t_flat[order], gates.reshape(-1)[order]
    counts = jnp.bincount(e_flat, length=N_EXPERTS)
    start = jnp.cumsum(counts) - counts
    padded = (counts + blk - 1) // blk * blk
    pend = jnp.cumsum(padded)
    pstart = pend - padded
    dest = pstart[e_s] + jnp.arange(nk) - start[e_s]
    cap = (nk + N_EXPERTS * (blk - 1) + blk - 1) // blk * blk
    n_blk = cap // blk
    rows_tok = jnp.full((cap,), n, jnp.int32).at[dest].set(t_s)
    rows_gate = jnp.zeros((cap,), jnp.float32).at[dest].set(g_s)
    blk_exp = jnp.minimum(jnp.searchsorted(pend, jnp.arange(n_blk) * blk, side='right'), N_EXPERTS - 1)
    x_pad = jnp.concatenate([xt, jnp.zeros((1, D_MODEL), xt.dtype)], axis=0)

    def expert_block(args):
        tok, e = args
        a, b = jnp.split(x_pad[tok] @ w_exp_gu[e], 2, axis=-1)
        return (jax.nn.silu(a) * b) @ w_exp_down[e]

    y_rows = lax.map(expert_block, (rows_tok.reshape(n_blk, blk), blk_exp)).reshape(cap, D_MODEL)
    y_rows = y_rows * rows_gate[:, None].astype(y_rows.dtype)
    routed = jax.ops.segment_sum(y_rows, rows_tok, num_segments=n + 1)[:n]
    sa, sb = jnp.split(xt @ w_sh_gu, 2, axis=-1)
    shared = (jax.nn.silu(sa) * sb) @ w_sh_down
    return (routed + shared).reshape(shp)


def trunk_layer(x, conv_buf, mem_k, mem_v, attend, moe_block, w_in, b_in, conv_w, conv_b, conv_ln_g,
                conv_ln_b, w_conv_out, w_attn_out, w_mem_out, w_out, ln1_g, ln1_b, w_router, router_bias,
                w_exp_gu, w_exp_down, w_sh_gu, w_sh_down, ln2_g, ln2_b):
    N, T, _ = x.shape
    z = x @ w_in + b_in
    glu_in, q, k, v, q_idx, k_idx, w_idx, q_mem, g = split_proj(z)
    q = q.reshape(N, T, N_HEADS, HEAD_DIM)
    k = k.reshape(N, T, N_HEADS, HEAD_DIM)
    v = v.reshape(N, T, N_HEADS, HEAD_DIM)
    q_idx = q_idx.reshape(N, T, N_IDX_HEADS, IDX_DIM)
    q_mem = q_mem.reshape(N, T, MEM_HEADS, MEM_HEAD_DIM)
    conv_out, new_buf = conv_branch(glu_in, conv_buf, conv_w, conv_b, conv_ln_g, conv_ln_b, w_conv_out)
    att_out = attend(q, k, v, q_idx, w_idx, k_idx).reshape(N, T, ATT_W) @ w_attn_out
    mem_out = mem_attend(q_mem, mem_k, mem_v) @ w_mem_out
    g_conv, g_att, g_mem = jnp.split(jax.nn.sigmoid(g), N_BRANCH, axis=-1)
    h = (g_conv * conv_out + g_att * att_out + g_mem * mem_out) @ w_out
    x1 = layer_norm(ALPHA * x + h, ln1_g, ln1_b)
    f = moe_ffn(x1, w_router, router_bias, w_exp_gu, w_exp_down, w_sh_gu, w_sh_down, moe_block)
    x2 = layer_norm(ALPHA * x1 + f, ln2_g, ln2_b)
    return x2, k, v, k_idx, new_buf


def setup_inputs(seed: int = 0) -> dict:
    key = jax.random.key(seed)
    ks = iter(jax.random.split(key, 40))

    def nrm(shape, scale):
        return jax.random.normal(next(ks), shape, jnp.float32) * scale

    n_pages = PAST_LEN // PAGE_SIZE
    n_pool = (DEC_BATCH * n_pages * 5) // 4
    L = DEPTH
    v_lo = 2 * D_CONV + 2 * ATT_W
    in_scale = jnp.ones((N_IN,), jnp.float32).at[v_lo:v_lo + ATT_W].set(BETA)
    mem_scale = jnp.concatenate([jnp.ones((MEM_W,), jnp.float32), jnp.full((MEM_W,), BETA, jnp.float32)])
    page_table = jax.random.permutation(next(ks), n_pool)[: DEC_BATCH * n_pages]
    page_table = page_table.reshape(DEC_BATCH, n_pages).astype(jnp.int32)
    return {
        'x_prompt': nrm((BATCH, SEQ, D_MODEL), 1.0),
        'x_sample': nrm((DEC_BATCH, DEC_SEQ, D_MODEL), 1.0),
        'mem_prompt': nrm((BATCH, N_MEM, D_MODEL), 1.0),
        'cache_k': nrm((L, n_pool, PAGE_SIZE, N_HEADS, HEAD_DIM), 1.0),
        'cache_v': nrm((L, n_pool, PAGE_SIZE, N_HEADS, HEAD_DIM), BETA),
        'cache_kidx': nrm((L, n_pool, PAGE_SIZE, IDX_DIM), 1.0),
        'cache_mem_k': nrm((L, DEC_BATCH, N_MEM, MEM_HEADS, MEM_HEAD_DIM), 1.0),
        'cache_mem_v': nrm((L, DEC_BATCH, N_MEM, MEM_HEADS, MEM_HEAD_DIM), BETA),
        'state_conv': nrm((L, DEC_BATCH, CONV_WIDTH - 1, D_CONV), 1.0),
        'page_table': page_table,
        'rel_bias': nrm((N_BUCKETS, N_HEADS), 0.1),
        'w_in': nrm((L, D_MODEL, N_IN), D_MODEL ** -0.5) * in_scale,
        'b_in': nrm((L, N_IN), 0.02),
        'conv_w': nrm((L, CONV_WIDTH, D_CONV), CONV_WIDTH ** -0.5),
        'conv_b': nrm((L, D_CONV), 0.02),
        'conv_ln_g': 1.0 + nrm((L, D_CONV), 0.02),
        'conv_ln_b': nrm((L, D_CONV), 0.02),
        'w_conv_out': nrm((L, D_CONV, D_MODEL), BETA * D_CONV ** -0.5),
        'w_attn_out': nrm((L, ATT_W, D_MODEL), BETA * ATT_W ** -0.5),
        'w_mem_kv': nrm((L, D_MODEL, 2 * MEM_W), D_MODEL ** -0.5) * mem_scale,
        'w_mem_out': nrm((L, MEM_W, D_MODEL), BETA * MEM_W ** -0.5),
        'w_out': nrm((L, D_MODEL, D_MODEL), BETA * D_MODEL ** -0.5),
        'ln1_g': 1.0 + nrm((L, D_MODEL), 0.02),
        'ln1_b': nrm((L, D_MODEL), 0.02),
        'w_router': nrm((L, D_MODEL, N_EXPERTS), D_MODEL ** -0.5),
        'router_bias': nrm((L, N_EXPERTS), 0.01),
        'w_exp_gu': nrm((L, N_EXPERTS, D_MODEL, 2 * D_EXPERT), D_MODEL ** -0.5),
        'w_exp_down': nrm((L, N_EXPERTS, D_EXPERT, D_MODEL), BETA * D_EXPERT ** -0.5),
        'w_sh_gu': nrm((L, D_MODEL, 2 * D_SHARED), D_MODEL ** -0.5),
        'w_sh_down': nrm((L, D_SHARED, D_MODEL), BETA * D_SHARED ** -0.5),
        'ln2_g': 1.0 + nrm((L, D_MODEL), 0.02),
        'ln2_b': nrm((L, D_MODEL), 0.02),
    }


def reference(x_prompt, x_sample, mem_prompt, cache_k, cache_v, cache_kidx, cache_mem_k, cache_mem_v,
              state_conv, page_table, rel_bias, w_in, b_in, conv_w, conv_b, conv_ln_g, conv_ln_b,
              w_conv_out, w_attn_out, w_mem_kv, w_mem_out, w_out, ln1_g, ln1_b, w_router, router_bias,
              w_exp_gu, w_exp_down, w_sh_gu, w_sh_down, ln2_g, ln2_b):
    B = x_prompt.shape[0]
    hp, hs = x_prompt, x_sample
    kp_l, vp_l, kip_l, mkp_l, mvp_l, cp_l = [], [], [], [], [], []
    ks_l, vs_l, kis_l, cs_l = [], [], [], []
    for l in range(DEPTH):
        lw = dict(w_in=w_in[l], b_in=b_in[l], conv_w=conv_w[l], conv_b=conv_b[l], conv_ln_g=conv_ln_g[l],
                  conv_ln_b=conv_ln_b[l], w_conv_out=w_conv_out[l], w_attn_out=w_attn_out[l],
                  w_mem_out=w_mem_out[l], w_out=w_out[l], ln1_g=ln1_g[l], ln1_b=ln1_b[l],
                  w_router=w_router[l], router_bias=router_bias[l], w_exp_gu=w_exp_gu[l],
                  w_exp_down=w_exp_down[l], w_sh_gu=w_sh_gu[l], w_sh_down=w_sh_down[l],
                  ln2_g=ln2_g[l], ln2_b=ln2_b[l])
        mk_p, mv_p = memory_kv(mem_prompt, w_mem_kv[l])
        conv0 = jnp.zeros((B, CONV_WIDTH - 1, D_CONV), hp.dtype)
        attend_p = functools.partial(prompt_dsa, rel_bias=rel_bias)
        hp, kp, vp, kip, cp = trunk_layer(hp, conv0, mk_p, mv_p, attend_p, PROMPT_MOE_BLOCK, **lw)
        kp_l.append(kp); vp_l.append(vp); kip_l.append(kip); mkp_l.append(mk_p); mvp_l.append(mv_p); cp_l.append(cp)
        attend_s = functools.partial(sample_dsa, cache_k=cache_k[l], cache_v=cache_v[l],
                                     cache_kidx=cache_kidx[l], page_table=page_table, rel_bias=rel_bias)
        hs, ks_, vs_, kis, cs = trunk_layer(hs, state_conv[l], cache_mem_k[l], cache_mem_v[l], attend_s,
                                            SAMPLE_MOE_BLOCK, **lw)
        ks_l.append(ks_); vs_l.append(vs_); kis_l.append(kis); cs_l.append(cs)
    return (hp, hs, jnp.stack(kp_l), jnp.stack(vp_l), jnp.stack(kip_l), jnp.stack(mkp_l), jnp.stack(mvp_l),
            jnp.stack(cp_l), jnp.stack(ks_l), jnp.stack(vs_l), jnp.stack(kis_l), jnp.stack(cs_l))
```

```python
import functools
import math

import jax
import jax.numpy as jnp
from jax import lax
from jax.experimental import pallas as pl
from jax.experimental.pallas import tpu as pltpu

F32 = jnp.float32
BF16 = jnp.bfloat16
I32 = jnp.int32

DEPTH = 1
N_IDX_HEADS = 8
TOPK_MAX = 256
N_BUCKETS = 32
REL_MAX_DIST = 128
TOP_K = 8
ROUTED_SCALE = 2.5
ALPHA = (2 * DEPTH) ** 0.25
LN_EPS = 1e-5
NEG_INF = -1e30

LANES = 128
SUBLANES = 8
VMEM_LIMIT = 56 * 1024 * 1024

INT_MIN = -(2 ** 31)
KEY_NEG_INF = -2139095041

_dot = functools.partial(jnp.dot, preferred_element_type=F32)


def _dot_nt(a, b):
    return lax.dot_general(a, b, (((1,), (1,)), ((), ())), preferred_element_type=F32)


def _pick_tile(n, cands=(512, 384, 256, 128, 64, 48, 32, 16, 8)):
    for c in cands:
        if n % c == 0:
            return c
    raise ValueError(f"no row tile divides {n}")


def _params(*sem):
    return pltpu.CompilerParams(dimension_semantics=sem, vmem_limit_bytes=VMEM_LIMIT)


def _trunc_bf16(x):
    bits = lax.bitcast_convert_type(x, jnp.uint32) & jnp.uint32(0xFFFF0000)
    return lax.bitcast_convert_type(bits, F32)


def _split2(x):
    hi = _trunc_bf16(x)
    return hi.astype(BF16), (x - hi).astype(BF16)


def _split3(x):
    hi = _trunc_bf16(x)
    r = x - hi
    mid = _trunc_bf16(r)
    lo = r - mid
    return hi.astype(BF16), mid.astype(BF16), lo.astype(BF16)


def _layer_norm(x, g, b):
    mu = jnp.mean(x, axis=-1, keepdims=True)
    xc = x - mu
    var = jnp.mean(xc * xc, axis=-1, keepdims=True)
    return xc * lax.rsqrt(var + LN_EPS) * g + b


def _silu(x):
    return x * jax.nn.sigmoid(x)


def _float_key(x):
    x = jnp.where(x == 0.0, 0.0, x)
    bits = lax.bitcast_convert_type(x, I32)
    return jnp.where(bits >= 0, bits, bits ^ jnp.int32(0x7FFFFFFF))


def _linear_kernel(x_ref, w_ref, b_ref, o_ref, *, split):
    x = x_ref[...]
    if split:
        xh, xl = _split2(x)
        wh, wl = _split2(w_ref[...])
        acc = _dot(xh, wh) + _dot(xl, wh) + _dot(xh, wl)
    else:
        acc = _dot(x.astype(BF16), w_ref[...])
    o_ref[...] = acc + b_ref[...]


def _linear(x, w, b, *, split=False, tn=512):
    m, k = x.shape
    n = w.shape[1]
    tm = _pick_tile(m)
    tn = min(tn, n)
    assert n % tn == 0
    return pl.pallas_call(
        functools.partial(_linear_kernel, split=split),
        grid=(m // tm, n // tn),
        in_specs=[pl.BlockSpec((tm, k), lambda i, j: (i, 0)),
                  pl.BlockSpec((k, tn), lambda i, j: (0, j)),
                  pl.BlockSpec((1, tn), lambda i, j: (0, j))],
        out_specs=pl.BlockSpec((tm, tn), lambda i, j: (i, j)),
        out_shape=jax.ShapeDtypeStruct((m, n), F32),
        compiler_params=_params("parallel", "parallel"),
        name="linear_split" if split else "linear",
    )(x, w, b)


CONV_HALO = 32
CONV_SUB = 64


def _glu(z, dc):
    return z[:, :dc] * jax.nn.sigmoid(z[:, dc:])


def _conv_prompt_kernel(glu_ref, prev_ref, buf_ref, cw_ref, cb_ref, g_ref, b_ref, y_ref, tail_ref, ext_ref,
                        *, ts, width, dc):
    i = pl.program_id(1)
    u = _glu(glu_ref[...], dc)
    ext_ref[0:CONV_HALO, :] = _glu(prev_ref[...], dc)

    @pl.when(i == 0)
    def _():
        ext_ref[CONV_HALO - (width - 1):CONV_HALO, :] = buf_ref[0]

    ext_ref[CONV_HALO:CONV_HALO + ts, :] = u
    tail_ref[0] = u[ts - CONV_HALO:, :]
    off = CONV_HALO - (width - 1)
    for r in range(ts // CONV_SUB):
        acc = jnp.broadcast_to(cb_ref[...], (CONV_SUB, dc))
        for j in range(width):
            lo = r * CONV_SUB + off + j
            acc = acc + ext_ref[lo:lo + CONV_SUB, :] * cw_ref[j:j + 1, :]
        y_ref[r * CONV_SUB:(r + 1) * CONV_SUB, :] = _silu(_layer_norm(acc, g_ref[...], b_ref[...]))


def _conv_prompt(z_main, conv_buf, conv_w, conv_b, ln_g, ln_b, nb, seq):
    width, dc = conv_w.shape
    assert width - 1 <= CONV_HALO
    ts = 256
    assert seq % ts == 0 and ts % CONV_SUB == 0 and ts % CONV_HALO == 0
    nt = seq // ts
    kern = functools.partial(_conv_prompt_kernel, ts=ts, width=width, dc=dc)
    return pl.pallas_call(
        kern,
        grid=(nb, nt),
        in_specs=[
            pl.BlockSpec((ts, 2 * dc), lambda n, i: (n * nt + i, 0)),
            pl.BlockSpec((CONV_HALO, 2 * dc),
                         lambda n, i: (jnp.maximum((n * nt + i) * (ts // CONV_HALO) - 1, 0), 0)),
            pl.BlockSpec((1, width - 1, dc), lambda n, i: (n, 0, 0)),
            pl.BlockSpec((width, dc), lambda n, i: (0, 0)),
            pl.BlockSpec((1, dc), lambda n, i: (0, 0)),
            pl.BlockSpec((1, dc), lambda n, i: (0, 0)),
            pl.BlockSpec((1, dc), lambda n, i: (0, 0)),
        ],
        out_specs=[pl.BlockSpec((ts, dc), lambda n, i: (n * nt + i, 0)),
                   pl.BlockSpec((1, CONV_HALO, dc), lambda n, i: (n, 0, 0))],
        out_shape=[jax.ShapeDtypeStruct((nb * seq, dc), F32),
                   jax.ShapeDtypeStruct((nb, CONV_HALO, dc), F32)],
        scratch_shapes=[pltpu.VMEM((CONV_HALO + ts, dc), F32)],
        compiler_params=_params("parallel", "arbitrary"),
        name="conv_prompt",
    )(z_main, z_main, conv_buf, conv_w, conv_b, ln_g, ln_b)


def _conv_sample_kernel(glu_ref, buf_ref, cw_ref, cb_ref, g_ref, b_ref, y_ref, new_ref, *, width, dc):
    u = _glu(glu_ref[...], dc)
    acc = cb_ref[...] + u * cw_ref[width - 1:width, :]
    for j in range(width - 1):
        row = buf_ref[j]
        acc = acc + row * cw_ref[j:j + 1, :]
        if j >= 1:
            new_ref[j - 1] = row
    new_ref[width - 2] = u
    y_ref[...] = _silu(_layer_norm(acc, g_ref[...], b_ref[...]))


def _conv_sample(z_main, buf_t, conv_w, conv_b, ln_g, ln_b):
    width, dc = conv_w.shape
    nb = z_main.shape[0]
    tb = _pick_tile(nb, (32, 16, 8))
    kern = functools.partial(_conv_sample_kernel, width=width, dc=dc)
    return pl.pallas_call(
        kern,
        grid=(nb // tb,),
        in_specs=[pl.BlockSpec((tb, 2 * dc), lambda i: (i, 0)),
                  pl.BlockSpec((width - 1, tb, dc), lambda i: (0, i, 0)),
                  pl.BlockSpec((width, dc), lambda i: (0, 0)),
                  pl.BlockSpec((1, dc), lambda i: (0, 0)),
                  pl.BlockSpec((1, dc), lambda i: (0, 0)),
                  pl.BlockSpec((1, dc), lambda i: (0, 0))],
        out_specs=[pl.BlockSpec((tb, dc), lambda i: (i, 0)),
                   pl.BlockSpec((width - 1, tb, dc), lambda i: (0, i, 0))],
        out_shape=[jax.ShapeDtypeStruct((nb, dc), F32),
                   jax.ShapeDtypeStruct((width - 1, nb, dc), F32)],
        compiler_params=_params("parallel"),
        name="conv_sample",
    )(z_main, buf_t, conv_w, conv_b, ln_g, ln_b)


def _mem_prompt_kernel(q_ref, mk_ref, mv_ref, o_ref, *, heads, hd):
    scale = hd ** -0.5
    for h in range(heads):
        sl = slice(h * hd, (h + 1) * hd)
        q = (q_ref[:, sl] * scale).astype(BF16)
        s = _dot_nt(q, mk_ref[:, sl].astype(BF16))
        m = jnp.max(s, axis=-1, keepdims=True)
        p = jnp.exp(s - m)
        l = jnp.sum(p, axis=-1, keepdims=True)
        o_ref[:, sl] = _dot(p.astype(BF16), mv_ref[:, sl].astype(BF16)) / l


def _mem_prompt(z_main, q_col, mk, mv, nb, seq, heads):
    w = mk.shape[1]
    n_mem = mk.shape[0] // nb
    tq = _pick_tile(seq, (512, 256, 128))
    nt = seq // tq
    kern = functools.partial(_mem_prompt_kernel, heads=heads, hd=w // heads)
    return pl.pallas_call(
        kern,
        grid=(nb, nt),
        in_specs=[pl.BlockSpec((tq, w), lambda n, i: (n * nt + i, q_col)),
                  pl.BlockSpec((n_mem, w), lambda n, i: (n, 0)),
                  pl.BlockSpec((n_mem, w), lambda n, i: (n, 0))],
        out_specs=pl.BlockSpec((tq, w), lambda n, i: (n * nt + i, 0)),
        out_shape=jax.ShapeDtypeStruct((nb * seq, w), F32),
        compiler_params=_params("parallel", "parallel"),
        name="mem_prompt",
    )(z_main, mk, mv)


MEM_GROUP = 8


def _head_mask(heads, hd, rows=SUBLANES):
    r = lax.broadcasted_iota(I32, (rows, heads * hd), 0)
    c = lax.broadcasted_iota(I32, (rows, heads * hd), 1)
    return (c >= r * hd) & (c < (r + 1) * hd)


def _mem_sample_kernel(q_ref, mk_ref, mv_ref, o_ref, *, heads, hd):
    scale = hd ** -0.5
    hm = _head_mask(heads, hd)
    for g in range(MEM_GROUP):
        q = q_ref[g:g + 1, :] * scale
        q_bd = jnp.where(hm, jnp.broadcast_to(q, hm.shape), 0.0).astype(BF16)
        s = _dot_nt(q_bd, mk_ref[g].astype(BF16))
        m = jnp.max(s, axis=-1, keepdims=True)
        p = jnp.exp(s - m)
        l = jnp.sum(p, axis=-1, keepdims=True)
        o = _dot(p.astype(BF16), mv_ref[g].astype(BF16)) / l
        o_ref[g:g + 1, :] = jnp.sum(jnp.where(hm, o, 0.0), axis=0, keepdims=True)


def _mem_sample(z_main, q_col, mk, mv, heads):
    nb, n_mem, w = mk.shape
    assert nb % MEM_GROUP == 0 and heads <= SUBLANES
    kern = functools.partial(_mem_sample_kernel, heads=heads, hd=w // heads)
    return pl.pallas_call(
        kern,
        grid=(nb // MEM_GROUP,),
        in_specs=[pl.BlockSpec((MEM_GROUP, w), lambda i: (i, q_col)),
                  pl.BlockSpec((MEM_GROUP, n_mem, w), lambda i: (i, 0, 0)),
                  pl.BlockSpec((MEM_GROUP, n_mem, w), lambda i: (i, 0, 0))],
        out_specs=pl.BlockSpec((MEM_GROUP, w), lambda i: (i, 0)),
        out_shape=jax.ShapeDtypeStruct((nb, w), F32),
        compiler_params=_params("parallel"),
        name="mem_sample",
    )(z_main, mk, mv)


def _count(load_chunk, n_chunks, chunk_w, rows, pred):
    lane = lax.broadcasted_iota(I32, (rows, chunk_w), 1)

    def body(c, acc):
        ind = jnp.where(pred(load_chunk(c), c * chunk_w + lane), 1.0, 0.0)
        part = ind[:, 0:LANES]
        for b in range(1, chunk_w // LANES):
            part = part + ind[:, b * LANES:(b + 1) * LANES]
        return acc + part

    acc = lax.fori_loop(0, n_chunks, body, jnp.zeros((rows, LANES), F32))
    return jnp.sum(acc, axis=-1, keepdims=True)


def _topk_select(load_chunk, n_chunks, chunk_w, rows, top):
    count = functools.partial(_count, load_chunk, n_chunks, chunk_w, rows)
    topf = float(top)

    def bit_step(s, thr):
        cand = jnp.where(s == 0, 0, thr | lax.shift_left(jnp.int32(1), 31 - s))
        cnt = count(lambda k, idx: k >= cand)
        return jnp.where(cnt >= topf, cand, thr)

    thr = lax.fori_loop(0, 32, bit_step, jnp.full((rows, 1), INT_MIN, I32))
    thr = jnp.maximum(thr, KEY_NEG_INF + 1)
    n_gt = count(lambda k, idx: k > thr)
    n_ge = count(lambda k, idx: k >= thr)
    need = topf - n_gt
    return thr, n_ge, need, count


def _tie_bound(count, thr, need, n_ge, top, idx_bits, rows):
    full = jnp.full((rows, 1), (1 << idx_bits) - 1, I32)

    def search():
        def bit_step(s, pos):
            cand = pos + lax.shift_left(jnp.int32(1), idx_bits - 1 - s)
            cnt = count(lambda k, idx: (k == thr) & (idx < cand))
            return jnp.where(cnt < need, cand, pos)
        return lax.fori_loop(0, idx_bits, bit_step, jnp.zeros((rows, 1), I32))

    excess = jnp.max(n_ge) > float(top)
    return lax.cond(excess, search, lambda: full)


DSA_BLOCK = 256
SEARCH_BLOCKS = 4


def _dsa_prompt_kernel(q_ref, q6_ref, w_ref, k6_ref, k_ref, v_ref, bias_ref, o_ref,
                       keys_ref, qm_ref, m_ref, l_ref, acc_ref, *, heads, hd, idx_heads, top, seq):
    blk = DSA_BLOCK
    i = pl.program_id(1)
    pair = LANES // hd
    n_search = (i + SEARCH_BLOCKS) // SEARCH_BLOCKS
    qpos = i * blk + lax.broadcasted_iota(I32, (blk, blk), 0)
    col = lax.broadcasted_iota(I32, (blk, blk), 1)

    def score_block(j, _):
        kc = k6_ref[pl.ds(pl.multiple_of(j * blk, blk), blk), :]
        sc = jnp.zeros((blk, blk), F32)
        for h in range(idx_heads):
            d = _dot_nt(q6_ref[0, h], kc)
            sc = sc + w_ref[:, h:h + 1] * jnp.maximum(d, 0.0)
        sc = jnp.where(j * blk + col <= qpos, sc, -jnp.inf)
        keys_ref[j] = _float_key(sc)
        return 0

    lax.fori_loop(0, n_search * SEARCH_BLOCKS, score_block, 0)

    def load_chunk(c):
        return jnp.concatenate([keys_ref[c * SEARCH_BLOCKS + b] for b in range(SEARCH_BLOCKS)], axis=-1)

    chunk_w = SEARCH_BLOCKS * blk
    thr, n_ge, need, count = _topk_select(load_chunk, n_search, chunk_w, blk, top)
    pos = _tie_bound(count, thr, need, n_ge, top, (seq - 1).bit_length(), blk)

    lane = lax.broadcasted_iota(I32, (blk, LANES), 1)
    for h in range(heads):
        g = h // pair
        qg = q_ref[:, g * LANES:(g + 1) * LANES] * (hd ** -0.5)
        own = (lane >= (h % pair) * hd) & (lane < (h % pair + 1) * hd)
        qm_ref[h] = jnp.where(own, qg, 0.0).astype(BF16)
    m_ref[...] = jnp.full(m_ref.shape, NEG_INF, F32)
    l_ref[...] = jnp.zeros(l_ref.shape, F32)
    acc_ref[...] = jnp.zeros(acc_ref.shape, F32)

    def attend(j, near):
        kb = keys_ref[j]
        idx = j * blk + col
        sel = (kb > thr) | ((kb == thr) & (idx <= pos))
        addm = jnp.where(sel, 0.0, NEG_INF)
        rows = pl.ds(pl.multiple_of(j * blk, blk), blk)
        for h in range(heads):
            g = h // pair
            s = _dot_nt(qm_ref[h], k_ref[rows, g * LANES:(g + 1) * LANES]) + addm
            if near is not None:
                s = s + bias_ref[near, h]
            m_old = m_ref[h]
            m_new = jnp.maximum(m_old, jnp.max(s, axis=-1, keepdims=True))
            p = jnp.exp(s - m_new)
            alpha = jnp.exp(m_old - m_new)
            l_ref[h] = alpha * l_ref[h] + jnp.sum(p, axis=-1, keepdims=True)
            acc_ref[h] = alpha * acc_ref[h] + _dot(p.astype(BF16), v_ref[rows, g * LANES:(g + 1) * LANES])
            m_ref[h] = m_new

    def far_block(j, _):
        attend(j, None)
        return 0

    lax.fori_loop(0, i - 1, far_block, 0)

    @pl.when(i >= 1)
    def _():
        attend(i - 1, 1)

    attend(i, 0)

    for g in range(heads // pair):
        out = jnp.zeros((blk, LANES), F32)
        for r in range(pair):
            h = g * pair + r
            own = (lane >= r * hd) & (lane < (r + 1) * hd)
            out = jnp.where(own, acc_ref[h] / l_ref[h], out)
        o_ref[:, g * LANES:(g + 1) * LANES] = out


def _dsa_prompt(z_main, q_col, q6, w_idx, k6, k_bf, v_bf, bias, nb, seq, heads, top):
    blk = DSA_BLOCK
    w = k_bf.shape[1]
    hd = w // heads
    idx_heads, kd = q6.shape[1], q6.shape[3]
    assert seq % (blk * SEARCH_BLOCKS) == 0 and LANES % hd == 0 and blk >= REL_MAX_DIST
    nq = seq // blk
    once = pl.Buffered(1)
    kern = functools.partial(_dsa_prompt_kernel, heads=heads, hd=hd, idx_heads=idx_heads, top=top, seq=seq)
    return pl.pallas_call(
        kern,
        grid=(nb, nq),
        in_specs=[
            pl.BlockSpec((blk, w), lambda n, i: (n * nq + i, q_col)),
            pl.BlockSpec((1, idx_heads, blk, kd), lambda n, i: (n, 0, i, 0)),
            pl.BlockSpec((blk, idx_heads), lambda n, i: (n * nq + i, 0)),
            pl.BlockSpec((seq, kd), lambda n, i: (n, 0), pipeline_mode=once),
            pl.BlockSpec((seq, w), lambda n, i: (n, 0), pipeline_mode=once),
            pl.BlockSpec((seq, w), lambda n, i: (n, 0), pipeline_mode=once),
            pl.BlockSpec((2, heads, blk, blk), lambda n, i: (0, 0, 0, 0), pipeline_mode=once),
        ],
        out_specs=pl.BlockSpec((blk, w), lambda n, i: (n * nq + i, 0)),
        out_shape=jax.ShapeDtypeStruct((nb * seq, w), F32),
        scratch_shapes=[pltpu.VMEM((nq, blk, blk), I32),
                        pltpu.VMEM((heads, blk, LANES), BF16),
                        pltpu.VMEM((heads, blk, 1), F32),
                        pltpu.VMEM((heads, blk, 1), F32),
                        pltpu.VMEM((heads, blk, LANES), F32)],
        compiler_params=_params("parallel", "arbitrary"),
        name="dsa_prompt",
    )(z_main, q6, w_idx, k6, k_bf, v_bf, bias)


def _dsa_sample_scores_kernel(pt_ref, q_ref, w_ref, kown_ref, *rest, n_pages, page):
    del pt_ref
    pages, o_ref = rest[:n_pages], rest[n_pages]
    q = q_ref[0]
    qh, qm, ql = _split3(q)
    w = w_ref[0]
    for p in range(n_pages):
        kh, km, kl = _split3(pages[p][0])
        d = (_dot_nt(qh, kh) + _dot_nt(qh, km) + _dot_nt(qm, kh)
             + _dot_nt(qh, kl) + _dot_nt(qm, km) + _dot_nt(ql, kh))
        sc = jnp.sum(w * jnp.maximum(d, 0.0), axis=0, keepdims=True)
        o_ref[0, :, p * page:(p + 1) * page] = sc
    d_own = jnp.sum(q * kown_ref[0], axis=-1, keepdims=True)
    s_own = jnp.sum(w * jnp.maximum(d_own, 0.0), axis=0, keepdims=True)
    lane = lax.broadcasted_iota(I32, (1, LANES), 1)
    o_ref[0, :, n_pages * page:] = jnp.where(lane == 0, s_own, -jnp.inf)


def _dsa_sample_scores(page_table, q_idx, w_idx, k_own, cache_kidx):
    nb, n_pages = page_table.shape
    page, idx_dim = cache_kidx.shape[1:]
    assert page % LANES == 0
    width = n_pages * page + LANES
    kern = functools.partial(_dsa_sample_scores_kernel, n_pages=n_pages, page=page)
    page_specs = [pl.BlockSpec((1, page, idx_dim), lambda n, pt, p=p: (pt[n * n_pages + p], 0, 0))
                  for p in range(n_pages)]
    grid_spec = pltpu.PrefetchScalarGridSpec(
        num_scalar_prefetch=1,
        grid=(nb,),
        in_specs=[pl.BlockSpec((1,) + q_idx.shape[1:], lambda n, pt: (n, 0, 0)),
                  pl.BlockSpec((1,) + w_idx.shape[1:], lambda n, pt: (n, 0, 0)),
                  pl.BlockSpec((1, 1, idx_dim), lambda n, pt: (n, 0, 0))] + page_specs,
        out_specs=pl.BlockSpec((1, 1, width), lambda n, pt: (n, 0, 0)),
    )
    return pl.pallas_call(
        kern,
        grid_spec=grid_spec,
        out_shape=jax.ShapeDtypeStruct((nb, 1, width), F32),
        compiler_params=_params("arbitrary"),
        name="dsa_sample_scores",
    )(page_table.reshape(-1), q_idx, w_idx, k_own, *([cache_kidx] * n_pages))


def _dsa_sample_mask_kernel(s_ref, o_ref, *, top, width):
    rows = s_ref.shape[0]
    keys = _float_key(s_ref[...])
    thr, n_ge, need, count = _topk_select(lambda c: keys, 1, width, rows, top)
    pos = _tie_bound(count, thr, need, n_ge, top, (width - 1).bit_length(), rows)
    idx = lax.broadcasted_iota(I32, keys.shape, 1)
    sel = (keys > thr) | ((keys == thr) & (idx <= pos))
    o_ref[...] = jnp.where(sel, 0.0, NEG_INF)


def _dsa_sample_mask(scores, top):
    rows, width = scores.shape
    return pl.pallas_call(
        functools.partial(_dsa_sample_mask_kernel, top=top, width=width),
        out_shape=jax.ShapeDtypeStruct((rows, width), F32),
        compiler_params=pltpu.CompilerParams(vmem_limit_bytes=VMEM_LIMIT),
        name="dsa_sample_mask",
    )(scores)


def _dsa_sample_attend_kernel(pt_ref, q_ref, kown_ref, vown_ref, mask_ref, bias_ref, *rest,
                              n_pages, page, heads, hd):
    del pt_ref
    kp, vp, o_ref = rest[:n_pages], rest[n_pages:2 * n_pages], rest[2 * n_pages]
    past = n_pages * page
    hm = _head_mask(heads, hd)
    q_bd = jnp.where(hm, jnp.broadcast_to(q_ref[0] * (hd ** -0.5), hm.shape), 0.0)
    q_bf = q_bd.astype(BF16)
    logits = []
    for p in range(n_pages):
        sl = slice(p * page, (p + 1) * page)
        logits.append(_dot_nt(q_bf, kp[p][0].astype(BF16)) + bias_ref[:, sl] + mask_ref[0, :, sl])
    s_own = (jnp.sum(q_bd * kown_ref[0], axis=-1, keepdims=True)
             + bias_ref[:, past:past + 1] + mask_ref[0, :, past:past + 1])
    m = s_own
    for s in logits:
        m = jnp.maximum(m, jnp.max(s, axis=-1, keepdims=True))
    p_own = jnp.exp(s_own - m)
    l = p_own
    acc = p_own * vown_ref[0]
    for p in range(n_pages):
        pr = jnp.exp(logits[p] - m)
        l = l + jnp.sum(pr, axis=-1, keepdims=True)
        acc = acc + _dot(pr.astype(BF16), vp[p][0].astype(BF16))
    o_ref[0] = jnp.sum(jnp.where(hm, acc / l, 0.0), axis=0, keepdims=True)


def _dsa_sample_attend(page_table, q, k_own, v_own, mask, bias, cache_k, cache_v, heads):
    nb, n_pages = page_table.shape
    page, w = cache_k.shape[1:]
    assert heads == SUBLANES
    width = mask.shape[-1]
    kern = functools.partial(_dsa_sample_attend_kernel, n_pages=n_pages, page=page, heads=heads, hd=w // heads)
    page_specs = [pl.BlockSpec((1, page, w), lambda n, pt, p=p: (pt[n * n_pages + p], 0, 0))
                  for p in range(n_pages)]
    row = pl.BlockSpec((1, 1, w), lambda n, pt: (n, 0, 0))
    grid_spec = pltpu.PrefetchScalarGridSpec(
        num_scalar_prefetch=1,
        grid=(nb,),
        in_specs=[row, row, row,
                  pl.BlockSpec((1, 1, width), lambda n, pt: (n, 0, 0)),
                  pl.BlockSpec((heads, width), lambda n, pt: (0, 0))] + page_specs + page_specs,
        out_specs=pl.BlockSpec((1, 1, w), lambda n, pt: (n, 0, 0)),
    )
    return pl.pallas_call(
        kern,
        grid_spec=grid_spec,
        out_shape=jax.ShapeDtypeStruct((nb, 1, w), F32),
        compiler_params=_params("arbitrary"),
        name="dsa_sample_attend",
    )(page_table.reshape(-1), q, k_own, v_own, mask, bias, *([cache_k] * n_pages), *([cache_v] * n_pages))


def _merge_kernel(x_ref, y_ref, a_ref, m_ref, g1_ref, g2_ref, g3_ref, wc_ref, wa_ref, wm_ref, wo_ref,
                  lg_ref, lb_ref, o_ref):
    c = _dot(y_ref[...].astype(BF16), wc_ref[...])
    a = _dot(a_ref[...].astype(BF16), wa_ref[...])
    m = _dot(m_ref[...].astype(BF16), wm_ref[...])
    merged = (jax.nn.sigmoid(g1_ref[...]) * c + jax.nn.sigmoid(g2_ref[...]) * a
              + jax.nn.sigmoid(g3_ref[...]) * m)
    h = _dot(merged.astype(BF16), wo_ref[...])
    o_ref[...] = _layer_norm(ALPHA * x_ref[...] + h, lg_ref[...], lb_ref[...])


def _merge(x, y_act, att, mem, z_main, g_col, wc, wa, wm, wo, ln_g, ln_b):
    n, d = x.shape
    tm = _pick_tile(n, (256, 128, 64, 32, 16, 8))
    row = lambda width: pl.BlockSpec((tm, width), lambda i: (i, 0))
    gate = lambda b: pl.BlockSpec((tm, d), lambda i: (i, g_col + b))
    full = lambda a: pl.BlockSpec(a.shape, lambda i: (0, 0))
    return pl.pallas_call(
        _merge_kernel,
        grid=(n // tm,),
        in_specs=[row(d), row(y_act.shape[1]), row(att.shape[1]), row(mem.shape[1]),
                  gate(0), gate(1), gate(2), full(wc), full(wa), full(wm), full(wo), full(ln_g), full(ln_b)],
        out_specs=row(d),
        out_shape=jax.ShapeDtypeStruct((n, d), F32),
        compiler_params=_params("parallel"),
        name="merge_ln1",
    )(x, y_act, att, mem, z_main, z_main, z_main, wc, wa, wm, wo, ln_g, ln_b)


MOE_BLOCK = 128
MOE_TOKENS = 128


def _router_kernel(x_ref, wh_ref, wl_ref, rb_ref, e_ref, g_ref, r_ref, cnt_ref, carry_ref, *, topk):
    i = pl.program_id(0)

    @pl.when(i == 0)
    def _():
        carry_ref[...] = jnp.zeros(carry_ref.shape, F32)

    xh, xl = _split2(x_ref[...])
    logits = _dot(xh, wh_ref[...]) + _dot(xl, wh_ref[...]) + _dot(xh, wl_ref[...])
    scores = jax.nn.sigmoid(logits)
    tm, ne = scores.shape
    sel = scores + rb_ref[...]
    lane = lax.broadcasted_iota(I32, (tm, ne), 1).astype(F32)
    chosen = jnp.zeros((tm, ne), F32)
    picks = []
    total = jnp.zeros((tm, 1), F32)
    for _ in range(topk):
        mx = jnp.max(sel, axis=-1, keepdims=True)
        idx = jnp.min(jnp.where(sel == mx, lane, float(ne)), axis=-1, keepdims=True)
        hot = lane == idx
        sk = jnp.sum(jnp.where(hot, scores, 0.0), axis=-1, keepdims=True)
        sel = jnp.where(hot, -jnp.inf, sel)
        chosen = jnp.where(hot, 1.0, chosen)
        total = total + sk
        picks.append((idx, sk, hot))
    r = lax.broadcasted_iota(I32, (tm, tm), 0)
    c = lax.broadcasted_iota(I32, (tm, tm), 1)
    before = jnp.where(c < r, 1.0, 0.0).astype(BF16)
    rank = _dot(before, chosen.astype(BF16)) + carry_ref[...]
    out_lane = lax.broadcasted_iota(I32, (tm, LANES), 1)
    e_out = jnp.zeros((tm, LANES), I32)
    g_out = jnp.zeros((tm, LANES), F32)
    r_out = jnp.zeros((tm, LANES), I32)
    for k, (idx, sk, hot) in enumerate(picks):
        rk = jnp.sum(jnp.where(hot, rank, 0.0), axis=-1, keepdims=True)
        e_out = jnp.where(out_lane == k, idx.astype(I32), e_out)
        g_out = jnp.where(out_lane == k, sk / total * ROUTED_SCALE, g_out)
        r_out = jnp.where(out_lane == k, rk.astype(I32), r_out)
    e_ref[...] = e_out
    g_ref[...] = g_out
    r_ref[...] = r_out
    carry_ref[...] = carry_ref[...] + jnp.sum(chosen, axis=0, keepdims=True)
    cnt_ref[...] = carry_ref[...]


def _router(x1, w_router, router_bias):
    n, d = x1.shape
    ne = w_router.shape[1]
    tm = _pick_tile(n, (256, 128, 64, 48, 32, 16, 8))
    wh, wl = _split2(w_router)
    out = lambda: pl.BlockSpec((tm, LANES), lambda i: (i, 0))
    return pl.pallas_call(
        functools.partial(_router_kernel, topk=TOP_K),
        grid=(n // tm,),
        in_specs=[pl.BlockSpec((tm, d), lambda i: (i, 0)),
                  pl.BlockSpec((d, ne), lambda i: (0, 0)),
                  pl.BlockSpec((d, ne), lambda i: (0, 0)),
                  pl.BlockSpec((1, ne), lambda i: (0, 0))],
        out_specs=[out(), out(), out(), pl.BlockSpec((1, ne), lambda i: (0, 0))],
        out_shape=[jax.ShapeDtypeStruct((n, LANES), I32), jax.ShapeDtypeStruct((n, LANES), F32),
                   jax.ShapeDtypeStruct((n, LANES), I32), jax.ShapeDtypeStruct((1, ne), F32)],
        scratch_shapes=[pltpu.VMEM((1, ne), F32)],
        compiler_params=_params("arbitrary"),
        name="router",
    )(x1, wh, wl, router_bias)


def _dispatch_kernel(dest_ref, x_ref, init_ref, xs_ref, sem, *, tt, topk):
    del init_ref

    def copy(t, d):
        return pltpu.make_async_copy(x_ref.at[pl.ds(t, 1)], xs_ref.at[pl.ds(d, 1)], sem)

    def start(t, _):
        for k in range(topk):
            copy(t, dest_ref[t * topk + k]).start()
        return 0

    lax.fori_loop(0, tt, start, 0)

    def wait(t, _):
        for k in range(topk):
            copy(t, dest_ref[t * topk + k]).wait()
        return 0

    lax.fori_loop(0, tt, wait, 0)


def _dispatch(x1, dest, cap):
    n, d = x1.shape
    topk = dest.shape[1]
    tt = _pick_tile(n, (MOE_TOKENS, 64, 48, 32, 16, 8))
    init = jnp.zeros((cap, d), F32)
    return pl.pallas_call(
        functools.partial(_dispatch_kernel, tt=tt, topk=topk),
        grid=(n // tt,),
        in_specs=[pl.BlockSpec((tt * topk,), lambda i: (i,), memory_space=pltpu.SMEM),
                  pl.BlockSpec((tt, d), lambda i: (i, 0)),
                  pl.BlockSpec(memory_space=pl.ANY)],
        out_specs=pl.BlockSpec(memory_space=pl.ANY),
        out_shape=jax.ShapeDtypeStruct((cap, d), F32),
        scratch_shapes=[pltpu.SemaphoreType.DMA(())],
        input_output_aliases={2: 0},
        compiler_params=_params("arbitrary"),
        name="moe_dispatch",
    )(dest.reshape(-1), x1, init)


def _expert_kernel(be_ref, na_ref, xs_ref, wgu_ref, wdn_ref, y_ref, wgu_bf, wdn_bf, *, de):
    b = pl.program_id(0)

    @pl.when(b < na_ref[0])
    def _():
        @pl.when((b == 0) | (be_ref[b] != be_ref[jnp.maximum(b - 1, 0)]))
        def _():
            wgu_bf[...] = wgu_ref[0].astype(BF16)
            wdn_bf[...] = wdn_ref[0].astype(BF16)

        h = _dot(xs_ref[...].astype(BF16), wgu_bf[...])
        act = _silu(h[:, :de]) * h[:, de:]
        y_ref[...] = _dot(act.astype(BF16), wdn_bf[...])

    @pl.when(b >= na_ref[0])
    def _():
        y_ref[...] = jnp.zeros(y_ref.shape, F32)


def _experts(xs, blk_exp, n_active, w_gu, w_down):
    cap, d = xs.shape
    ne, _, de2 = w_gu.shape
    de = de2 // 2
    n_blk = cap // MOE_BLOCK
    grid_spec = pltpu.PrefetchScalarGridSpec(
        num_scalar_prefetch=2,
        grid=(n_blk,),
        in_specs=[pl.BlockSpec((MOE_BLOCK, d), lambda b, be, na: (b, 0)),
                  pl.BlockSpec((1, d, de2), lambda b, be, na: (be[b], 0, 0)),
                  pl.BlockSpec((1, de, d), lambda b, be, na: (be[b], 0, 0))],
        out_specs=pl.BlockSpec((MOE_BLOCK, d), lambda b, be, na: (b, 0)),
        scratch_shapes=[pltpu.VMEM((d, de2), BF16), pltpu.VMEM((de, d), BF16)],
    )
    return pl.pallas_call(
        functools.partial(_expert_kernel, de=de),
        grid_spec=grid_spec,
        out_shape=jax.ShapeDtypeStruct((cap, d), F32),
        compiler_params=_params("arbitrary"),
        name="moe_experts",
    )(blk_exp, n_active, xs, w_gu, w_down)


def _combine_kernel(dest_ref, gate_ref, x_ref, y_ref, wgu_ref, wdn_ref, lg_ref, lb_ref, o_ref, buf, sem,
                    *, tt, topk, ds):
    def copy(t, k):
        return pltpu.make_async_copy(y_ref.at[pl.ds(dest_ref[t * topk + k], 1)], buf.at[k, pl.ds(t, 1)], sem)

    def start(t, _):
        for k in range(topk):
            copy(t, k).start()
        return 0

    lax.fori_loop(0, tt, start, 0)

    x = x_ref[...]
    h = _dot(x.astype(BF16), wgu_ref[...])
    shared = _dot((_silu(h[:, :ds]) * h[:, ds:]).astype(BF16), wdn_ref[...])

    def wait(t, _):
        for k in range(topk):
            copy(t, k).wait()
        return 0

    lax.fori_loop(0, tt, wait, 0)

    routed = jnp.zeros(x.shape, F32)
    for k in range(topk):
        routed = routed + gate_ref[:, k:k + 1] * buf[k]
    o_ref[...] = _layer_norm(ALPHA * x + routed + shared, lg_ref[...], lb_ref[...])


def _combine(x1, y_rows, dest, gates, w_sh_gu, w_sh_down, ln_g, ln_b):
    n, d = x1.shape
    topk = dest.shape[1]
    tt = _pick_tile(n, (MOE_TOKENS, 64, 48, 32, 16, 8))
    full = lambda a: pl.BlockSpec(a.shape, lambda i: (0, 0))
    return pl.pallas_call(
        functools.partial(_combine_kernel, tt=tt, topk=topk, ds=w_sh_down.shape[0]),
        grid=(n // tt,),
        in_specs=[pl.BlockSpec((tt * topk,), lambda i: (i,), memory_space=pltpu.SMEM),
                  pl.BlockSpec((tt, LANES), lambda i: (i, 0)),
                  pl.BlockSpec((tt, d), lambda i: (i, 0)),
                  pl.BlockSpec(memory_space=pl.ANY),
                  full(w_sh_gu), full(w_sh_down), full(ln_g), full(ln_b)],
        out_specs=pl.BlockSpec((tt, d), lambda i: (i, 0)),
        out_shape=jax.ShapeDtypeStruct((n, d), F32),
        scratch_shapes=[pltpu.VMEM((topk, tt, d), F32), pltpu.SemaphoreType.DMA(())],
        compiler_params=_params("arbitrary"),
        name="moe_combine",
    )(dest.reshape(-1), gates, x1, y_rows, w_sh_gu, w_sh_down, ln_g, ln_b)


def _moe_ln2(x1, w_router, router_bias, w_exp_gu, w_exp_down, w_sh_gu, w_sh_down, ln_g, ln_b):
    n, d = x1.shape
    ne = w_router.shape[1]
    top_e, gates, rank, counts = _router(x1, w_router, router_bias)
    counts = counts[0].astype(I32)
    padded = (counts + MOE_BLOCK - 1) // MOE_BLOCK * MOE_BLOCK
    pend = jnp.cumsum(padded)
    pstart = pend - padded
    dest = pstart[top_e[:, :TOP_K]] + rank[:, :TOP_K]
    cap = (n * TOP_K + ne * (MOE_BLOCK - 1) + MOE_BLOCK - 1) // MOE_BLOCK * MOE_BLOCK
    n_blk = cap // MOE_BLOCK
    blk_exp = jnp.minimum(jnp.searchsorted(pend, jnp.arange(n_blk, dtype=I32) * MOE_BLOCK, side='right'),
                          ne - 1).astype(I32)
    n_active = (pend[-1:] // MOE_BLOCK).astype(I32)
    xs = _dispatch(x1, dest, cap)
    y_rows = _experts(xs, blk_exp, n_active, w_exp_gu, w_exp_down)
    return _combine(x1, y_rows, dest, gates, w_sh_gu.astype(BF16), w_sh_down.astype(BF16), ln_g, ln_b)


def _rel_bucket(dist):
    n = jnp.maximum(dist, 0)
    max_exact = N_BUCKETS // 2
    nf = jnp.maximum(n, max_exact).astype(F32)
    large = max_exact + (jnp.log(nf / max_exact) / math.log(REL_MAX_DIST / max_exact)
                         * (N_BUCKETS - max_exact)).astype(I32)
    return jnp.where(n < max_exact, n, jnp.minimum(large, N_BUCKETS - 1))


def kernel(x_prompt, x_sample, mem_prompt, cache_k, cache_v, cache_kidx, cache_mem_k, cache_mem_v, state_conv, page_table, rel_bias, w_in, b_in, conv_w, conv_b, conv_ln_g, conv_ln_b, w_conv_out, w_attn_out, w_mem_kv, w_mem_out, w_out, ln1_g, ln1_b, w_router, router_bias, w_exp_gu, w_exp_down, w_sh_gu, w_sh_down, ln2_g, ln2_b):
    assert w_in.shape[0] == DEPTH == 1
    nb, seq, d = x_prompt.shape
    nsb, nst, _ = x_sample.shape
    assert nst == 1
    width, dc = conv_w.shape[1:]
    _, n_pool, page, heads, hd = cache_k.shape
    idx_dim = cache_kidx.shape[-1]
    n_mem, mem_heads, mem_hd = cache_mem_k.shape[2:]
    aw, mw, iw = heads * hd, mem_heads * mem_hd, N_IDX_HEADS * idx_dim
    assert 2 * dc == d and aw * 2 == d and mw * 2 == d

    o_q = 2 * dc
    o_qi = o_q + 3 * aw
    o_ki = o_qi + iw
    o_wi = o_ki + idx_dim
    o_qm = o_wi + N_IDX_HEADS
    o_g = o_qm + mw
    w0, b0 = w_in[0], b_in[0]
    w_main = jnp.concatenate([w0[:, :o_qi], w0[:, o_qm:]], axis=1).astype(BF16)
    b_main = jnp.concatenate([b0[:o_qi], b0[o_qm:]])[None, :]
    n_idx = o_qm - o_qi
    n_idx_pad = -(-n_idx // LANES) * LANES
    w_idx = jnp.pad(w0[:, o_qi:o_qm], ((0, 0), (0, n_idx_pad - n_idx)))
    b_idx = jnp.pad(b0[o_qi:o_qm], (0, n_idx_pad - n_idx))[None, :]
    col_q, col_k, col_v, col_qm, col_g = 2, 3, 4, 5, 3

    xp = x_prompt.reshape(nb * seq, d)
    xs = x_sample.reshape(nsb, d)
    zp = _linear(xp, w_main, b_main)
    zs = _linear(xs, w_main, b_main)
    zip_ = _linear(xp, w_idx, b_idx, split=True, tn=n_idx_pad)
    zis = _linear(xs, w_idx, b_idx, split=True, tn=n_idx_pad)

    kp, vp = zp[:, col_k * aw:(col_k + 1) * aw], zp[:, col_v * aw:(col_v + 1) * aw]
    ks, vs = zs[:, col_k * aw:(col_k + 1) * aw], zs[:, col_v * aw:(col_v + 1) * aw]
    kip, kis = zip_[:, iw:iw + idx_dim], zis[:, iw:iw + idx_dim]

    conv0 = jnp.zeros((nb, width - 1, dc), F32)
    cw, cb, cg, cbb = conv_w[0], conv_b[0][None, :], conv_ln_g[0][None, :], conv_ln_b[0][None, :]
    yp, tail_p = _conv_prompt(zp, conv0, cw, cb, cg, cbb, nb, seq)
    conv_p = tail_p[:, CONV_HALO - (width - 1):, :]
    ys, new_t = _conv_sample(zs, jnp.swapaxes(state_conv[0], 0, 1), cw, cb, cg, cbb)
    conv_s = jnp.swapaxes(new_t, 0, 1)

    wkv = w_mem_kv[0].astype(BF16)
    mkv = _linear(mem_prompt.reshape(nb * n_mem, d), wkv, jnp.zeros((1, 2 * mw), F32))
    mk_p, mv_p = mkv[:, :mw], mkv[:, mw:]
    mem_p = _mem_prompt(zp, col_qm, mk_p, mv_p, nb, seq, mem_heads)
    mem_s = _mem_sample(zs, col_qm, cache_mem_k[0].reshape(nsb, n_mem, mw),
                        cache_mem_v[0].reshape(nsb, n_mem, mw), mem_heads)

    far = rel_bias[_rel_bucket(jnp.int32(4 * REL_MAX_DIST))]
    a = jnp.arange(DSA_BLOCK, dtype=I32)
    dist = jnp.stack([a[:, None] - a[None, :], DSA_BLOCK + a[:, None] - a[None, :]])
    bias_p = jnp.moveaxis(rel_bias[_rel_bucket(dist)] - far, -1, 1)
    qh, qm_, ql = _split3(zip_[:, :iw].reshape(nb, seq, N_IDX_HEADS, idx_dim))
    q6 = jnp.swapaxes(jnp.concatenate([qh, qh, qm_, qh, qm_, ql], axis=-1), 1, 2)
    kh, km, kl = _split3(kip)
    k6 = jnp.concatenate([kh, km, kh, kl, km, kh], axis=-1)
    top_p = min(TOPK_MAX, seq // 4)
    att_p = _dsa_prompt(zp, col_q, q6, zip_[:, iw + idx_dim:iw + idx_dim + N_IDX_HEADS], k6,
                        kp.astype(BF16), vp.astype(BF16), bias_p, nb, seq, heads, top_p)

    n_pages = page_table.shape[1]
    past = n_pages * page
    top_s = min(TOPK_MAX, (past + 1) // 4)
    scores = _dsa_sample_scores(page_table, zis[:, :iw].reshape(nsb, N_IDX_HEADS, idx_dim),
                                zis[:, iw + idx_dim:iw + idx_dim + N_IDX_HEADS].reshape(nsb, N_IDX_HEADS, 1),
                                kis.reshape(nsb, 1, idx_dim), cache_kidx[0])
    mask_s = _dsa_sample_mask(scores.reshape(nsb, past + LANES), top_s)
    spos = jnp.arange(past + LANES, dtype=I32)
    bias_s = rel_bias[_rel_bucket(past - spos)].T
    att_s = _dsa_sample_attend(page_table, zs[:, col_q * aw:(col_q + 1) * aw].reshape(nsb, 1, aw),
                               ks.reshape(nsb, 1, aw), vs.reshape(nsb, 1, aw),
                               mask_s.reshape(nsb, 1, past + LANES), bias_s,
                               cache_k[0].reshape(n_pool, page, aw), cache_v[0].reshape(n_pool, page, aw),
                               heads).reshape(nsb, aw)

    wc, wa, wm, wo = (w_conv_out[0].astype(BF16), w_attn_out[0].astype(BF16),
                      w_mem_out[0].astype(BF16), w_out[0].astype(BF16))
    l1g, l1b = ln1_g[0][None, :], ln1_b[0][None, :]
    x1p = _merge(xp, yp, att_p, mem_p, zp, col_g, wc, wa, wm, wo, l1g, l1b)
    x1s = _merge(xs, ys, att_s, mem_s, zs, col_g, wc, wa, wm, wo, l1g, l1b)
    x1 = jnp.concatenate([x1p, x1s], axis=0)
    x2 = _moe_ln2(x1, w_router[0], router_bias[0][None, :], w_exp_gu[0], w_exp_down[0],
                  w_sh_gu[0], w_sh_down[0], ln2_g[0][None, :], ln2_b[0][None, :])

    return (x2[:nb * seq].reshape(nb, seq, d), x2[nb * seq:].reshape(nsb, 1, d),
            kp.reshape(1, nb, seq, heads, hd), vp.reshape(1, nb, seq, heads, hd),
            kip.reshape(1, nb, seq, idx_dim),
            mk_p.reshape(1, nb, n_mem, mem_heads, mem_hd), mv_p.reshape(1, nb, n_mem, mem_heads, mem_hd),
            conv_p[None],
            ks.reshape(1, nsb, 1, heads, hd), vs.reshape(1, nsb, 1, heads, hd),
            kis.reshape(1, nsb, 1, idx_dim), conv_s[None])
```

```python
import functools
import math

import jax
import jax.numpy as jnp
from jax import lax
from jax.experimental import pallas as pl
from jax.experimental.pallas import tpu as pltpu

F32 = jnp.float32
BF16 = jnp.bfloat16
I32 = jnp.int32

DEPTH = 1
N_IDX_HEADS = 8
TOPK_MAX = 256
N_BUCKETS = 32
REL_MAX_DIST = 128
TOP_K = 8
ROUTED_SCALE = 2.5
ALPHA = (2 * DEPTH) ** 0.25
LN_EPS = 1e-5
NEG_INF = -1e30
LOG2E = math.log2(math.e)

LANES = 128
SUBLANES = 8
VMEM_LIMIT = 56 * 1024 * 1024

INT_MIN = -(2 ** 31)
KEY_NEG_INF = -2139095041

_dot = functools.partial(jnp.dot, preferred_element_type=F32)


def _dot_nt(a, b):
    return lax.dot_general(a, b, (((1,), (1,)), ((), ())), preferred_element_type=F32)


def _pick_tile(n, cands=(512, 384, 256, 128, 64, 48, 32, 16, 8)):
    for c in cands:
        if n % c == 0:
            return c
    raise ValueError(f"no row tile divides {n}")


def _params(*sem):
    return pltpu.CompilerParams(dimension_semantics=sem, vmem_limit_bytes=VMEM_LIMIT)


def _trunc_bf16(x):
    bits = lax.bitcast_convert_type(x, jnp.uint32) & jnp.uint32(0xFFFF0000)
    return lax.bitcast_convert_type(bits, F32)


def _split2(x):
    hi = _trunc_bf16(x)
    return hi.astype(BF16), (x - hi).astype(BF16)


def _split3(x):
    hi = _trunc_bf16(x)
    r = x - hi
    mid = _trunc_bf16(r)
    lo = r - mid
    return hi.astype(BF16), mid.astype(BF16), lo.astype(BF16)


def _layer_norm(x, g, b):
    mu = jnp.mean(x, axis=-1, keepdims=True)
    xc = x - mu
    var = jnp.mean(xc * xc, axis=-1, keepdims=True)
    return xc * lax.rsqrt(var + LN_EPS) * g + b


def _silu(x):
    return x * jax.nn.sigmoid(x)


def _float_key(x):
    x = jnp.where(x == 0.0, 0.0, x)
    bits = lax.bitcast_convert_type(x, I32)
    return jnp.where(bits >= 0, bits, bits ^ jnp.int32(0x7FFFFFFF))


def _linear_kernel(x_ref, w_ref, b_ref, o_ref, *, split):
    x = x_ref[...]
    if split:
        xh, xl = _split2(x)
        wh, wl = _split2(w_ref[...])
        acc = _dot(xh, wh) + _dot(xl, wh) + _dot(xh, wl)
    else:
        acc = _dot(x.astype(BF16), w_ref[...])
    o_ref[...] = acc + b_ref[...]


def _linear(x, w, b, *, split=False, tn=512):
    m, k = x.shape
    n = w.shape[1]
    tm = _pick_tile(m)
    tn = min(tn, n)
    assert n % tn == 0
    return pl.pallas_call(
        functools.partial(_linear_kernel, split=split),
        grid=(m // tm, n // tn),
        in_specs=[pl.BlockSpec((tm, k), lambda i, j: (i, 0)),
                  pl.BlockSpec((k, tn), lambda i, j: (0, j)),
                  pl.BlockSpec((1, tn), lambda i, j: (0, j))],
        out_specs=pl.BlockSpec((tm, tn), lambda i, j: (i, j)),
        out_shape=jax.ShapeDtypeStruct((m, n), F32),
        compiler_params=_params("parallel", "parallel"),
        name="linear_split" if split else "linear",
    )(x, w, b)


CONV_HALO = 32
CONV_SUB = 64


def _glu(z, dc):
    return z[:, :dc] * jax.nn.sigmoid(z[:, dc:])


def _conv_prompt_kernel(glu_ref, prev_ref, buf_ref, cw_ref, cb_ref, g_ref, b_ref, y_ref, tail_ref, ext_ref,
                        *, ts, width, dc):
    i = pl.program_id(1)
    u = _glu(glu_ref[...], dc)
    ext_ref[0:CONV_HALO, :] = _glu(prev_ref[...], dc)

    @pl.when(i == 0)
    def _():
        ext_ref[CONV_HALO - (width - 1):CONV_HALO, :] = buf_ref[0]

    ext_ref[CONV_HALO:CONV_HALO + ts, :] = u
    tail_ref[0] = u[ts - CONV_HALO:, :]
    off = CONV_HALO - (width - 1)
    for r in range(ts // CONV_SUB):
        acc = jnp.broadcast_to(cb_ref[...], (CONV_SUB, dc))
        for j in range(width):
            lo = r * CONV_SUB + off + j
            acc = acc + ext_ref[lo:lo + CONV_SUB, :] * cw_ref[j:j + 1, :]
        y_ref[r * CONV_SUB:(r + 1) * CONV_SUB, :] = _silu(_layer_norm(acc, g_ref[...], b_ref[...]))


def _conv_prompt(z_main, conv_buf, conv_w, conv_b, ln_g, ln_b, nb, seq):
    width, dc = conv_w.shape
    assert width - 1 <= CONV_HALO
    ts = 256
    assert seq % ts == 0 and ts % CONV_SUB == 0 and ts % CONV_HALO == 0
    nt = seq // ts
    kern = functools.partial(_conv_prompt_kernel, ts=ts, width=width, dc=dc)
    return pl.pallas_call(
        kern,
        grid=(nb, nt),
        in_specs=[
            pl.BlockSpec((ts, 2 * dc), lambda n, i: (n * nt + i, 0)),
            pl.BlockSpec((CONV_HALO, 2 * dc),
                         lambda n, i: (jnp.maximum((n * nt + i) * (ts // CONV_HALO) - 1, 0), 0)),
            pl.BlockSpec((1, width - 1, dc), lambda n, i: (n, 0, 0)),
            pl.BlockSpec((width, dc), lambda n, i: (0, 0)),
            pl.BlockSpec((1, dc), lambda n, i: (0, 0)),
            pl.BlockSpec((1, dc), lambda n, i: (0, 0)),
            pl.BlockSpec((1, dc), lambda n, i: (0, 0)),
        ],
        out_specs=[pl.BlockSpec((ts, dc), lambda n, i: (n * nt + i, 0)),
                   pl.BlockSpec((1, CONV_HALO, dc), lambda n, i: (n, 0, 0))],
        out_shape=[jax.ShapeDtypeStruct((nb * seq, dc), F32),
                   jax.ShapeDtypeStruct((nb, CONV_HALO, dc), F32)],
        scratch_shapes=[pltpu.VMEM((CONV_HALO + ts, dc), F32)],
        compiler_params=_params("parallel", "arbitrary"),
        name="conv_prompt",
    )(z_main, z_main, conv_buf, conv_w, conv_b, ln_g, ln_b)


def _conv_sample_kernel(glu_ref, buf_ref, cw_ref, cb_ref, g_ref, b_ref, y_ref, new_ref, *, width, dc):
    u = _glu(glu_ref[...], dc)
    acc = cb_ref[...] + u * cw_ref[width - 1:width, :]
    for j in range(width - 1):
        row = buf_ref[j]
        acc = acc + row * cw_ref[j:j + 1, :]
        if j >= 1:
            new_ref[j - 1] = row
    new_ref[width - 2] = u
    y_ref[...] = _silu(_layer_norm(acc, g_ref[...], b_ref[...]))


def _conv_sample(z_main, buf_t, conv_w, conv_b, ln_g, ln_b):
    width, dc = conv_w.shape
    nb = z_main.shape[0]
    tb = _pick_tile(nb, (32, 16, 8))
    kern = functools.partial(_conv_sample_kernel, width=width, dc=dc)
    return pl.pallas_call(
        kern,
        grid=(nb // tb,),
        in_specs=[pl.BlockSpec((tb, 2 * dc), lambda i: (i, 0)),
                  pl.BlockSpec((width - 1, tb, dc), lambda i: (0, i, 0)),
                  pl.BlockSpec((width, dc), lambda i: (0, 0)),
                  pl.BlockSpec((1, dc), lambda i: (0, 0)),
                  pl.BlockSpec((1, dc), lambda i: (0, 0)),
                  pl.BlockSpec((1, dc), lambda i: (0, 0))],
        out_specs=[pl.BlockSpec((tb, dc), lambda i: (i, 0)),
                   pl.BlockSpec((width - 1, tb, dc), lambda i: (0, i, 0))],
        out_shape=[jax.ShapeDtypeStruct((nb, dc), F32),
                   jax.ShapeDtypeStruct((width - 1, nb, dc), F32)],
        compiler_params=_params("parallel"),
        name="conv_sample",
    )(z_main, buf_t, conv_w, conv_b, ln_g, ln_b)


def _mem_prompt_kernel(q_ref, mk_ref, mv_ref, o_ref, *, heads, hd):
    scale = hd ** -0.5
    for h in range(heads):
        sl = slice(h * hd, (h + 1) * hd)
        q = (q_ref[:, sl] * scale).astype(BF16)
        s = _dot_nt(q, mk_ref[:, sl].astype(BF16))
        m = jnp.max(s, axis=-1, keepdims=True)
        p = jnp.exp(s - m)
        l = jnp.sum(p, axis=-1, keepdims=True)
        o_ref[:, sl] = _dot(p.astype(BF16), mv_ref[:, sl].astype(BF16)) / l


def _mem_prompt(z_main, q_col, mk, mv, nb, seq, heads):
    w = mk.shape[1]
    n_mem = mk.shape[0] // nb
    tq = _pick_tile(seq, (512, 256, 128))
    nt = seq // tq
    kern = functools.partial(_mem_prompt_kernel, heads=heads, hd=w // heads)
    return pl.pallas_call(
        kern,
        grid=(nb, nt),
        in_specs=[pl.BlockSpec((tq, w), lambda n, i: (n * nt + i, q_col)),
                  pl.BlockSpec((n_mem, w), lambda n, i: (n, 0)),
                  pl.BlockSpec((n_mem, w), lambda n, i: (n, 0))],
        out_specs=pl.BlockSpec((tq, w), lambda n, i: (n * nt + i, 0)),
        out_shape=jax.ShapeDtypeStruct((nb * seq, w), F32),
        compiler_params=_params("parallel", "parallel"),
        name="mem_prompt",
    )(z_main, mk, mv)


MEM_GROUP = 8


def _head_mask(heads, hd, rows=SUBLANES):
    r = lax.broadcasted_iota(I32, (rows, heads * hd), 0)
    c = lax.broadcasted_iota(I32, (rows, heads * hd), 1)
    return (c >= r * hd) & (c < (r + 1) * hd)


def _mem_sample_kernel(q_ref, mk_ref, mv_ref, o_ref, *, heads, hd):
    scale = hd ** -0.5
    hm = _head_mask(heads, hd)
    for g in range(MEM_GROUP):
        q = q_ref[g:g + 1, :] * scale
        q_bd = jnp.where(hm, jnp.broadcast_to(q, hm.shape), 0.0).astype(BF16)
        s = _dot_nt(q_bd, mk_ref[g].astype(BF16))
        m = jnp.max(s, axis=-1, keepdims=True)
        p = jnp.exp(s - m)
        l = jnp.sum(p, axis=-1, keepdims=True)
        o = _dot(p.astype(BF16), mv_ref[g].astype(BF16)) / l
        o_ref[g:g + 1, :] = jnp.sum(jnp.where(hm, o, 0.0), axis=0, keepdims=True)


def _mem_sample(z_main, q_col, mk, mv, heads):
    nb, n_mem, w = mk.shape
    assert nb % MEM_GROUP == 0 and heads <= SUBLANES
    kern = functools.partial(_mem_sample_kernel, heads=heads, hd=w // heads)
    return pl.pallas_call(
        kern,
        grid=(nb // MEM_GROUP,),
        in_specs=[pl.BlockSpec((MEM_GROUP, w), lambda i: (i, q_col)),
                  pl.BlockSpec((MEM_GROUP, n_mem, w), lambda i: (i, 0, 0)),
                  pl.BlockSpec((MEM_GROUP, n_mem, w), lambda i: (i, 0, 0))],
        out_specs=pl.BlockSpec((MEM_GROUP, w), lambda i: (i, 0)),
        out_shape=jax.ShapeDtypeStruct((nb, w), F32),
        compiler_params=_params("parallel"),
        name="mem_sample",
    )(z_main, mk, mv)


def _topk_select(count, shape, top):
    topf = float(top)

    def bit_step(s, thr):
        cand = jnp.where(s == 0, 0, thr | lax.shift_left(jnp.int32(1), 31 - s))
        cnt = count(lambda k, idx: k >= cand)
        return jnp.where(cnt >= topf, cand, thr)

    thr = lax.fori_loop(0, 32, bit_step, jnp.full(shape, INT_MIN, I32))
    thr = jnp.maximum(thr, KEY_NEG_INF + 1)
    n_gt = count(lambda k, idx: k > thr)
    n_ge = count(lambda k, idx: k >= thr)
    return thr, n_ge, topf - n_gt


def _tie_bound(count, shape, thr, need, n_ge, top, idx_bits):
    full = jnp.full(shape, (1 << idx_bits) - 1, I32)

    def search():
        def bit_step(s, pos):
            cand = pos + lax.shift_left(jnp.int32(1), idx_bits - 1 - s)
            cnt = count(lambda k, idx: (k == thr) & (idx < cand))
            return jnp.where(cnt < need, cand, pos)
        return lax.fori_loop(0, idx_bits, bit_step, jnp.zeros(shape, I32))

    excess = jnp.max(n_ge) > float(top)
    return lax.cond(excess, search, lambda: full)


DSA_BLOCK = 256
SEARCH_BLOCKS = 2


I16 = jnp.int16
I16_MIN = -(2 ** 15)
HI_NEG_INF = KEY_NEG_INF >> 16
LO_NEG_INF = ((KEY_NEG_INF & 0xFFFF) ^ 0x8000) - 0x10000 * (((KEY_NEG_INF & 0xFFFF) ^ 0x8000) >> 15)
PACKED_ROWS = 2 * SUBLANES


def _search16(count_ge, need, shape):
    def step(s, t):
        cand = jnp.where(s == 0, 0, t | lax.shift_left(jnp.int32(1), 15 - s))
        return jnp.where(count_ge(cand.astype(I16)) >= need, cand, t)
    return lax.fori_loop(0, 16, step, jnp.full(shape, I16_MIN, I32))


def _dsa_prompt_kernel(qt_ref, q6t_ref, wt_ref, k6_ref, k_ref, vt_ref, bias_ref, o_ref,
                       mask_ref, hi_ref, lo_ref, qm_ref, m_ref, acc_ref, s_ref, p_ref, a_ref,
                       *, heads, hd, idx_heads, top, seq):
    blk = DSA_BLOCK
    i = pl.program_id(1)
    pair = LANES // hd
    n_search = (i + SEARCH_BLOCKS) // SEARCH_BLOCKS
    krow = lax.broadcasted_iota(I32, (blk, blk), 0)
    qpos = i * blk + lax.broadcasted_iota(I32, (blk, blk), 1)
    shape = (1, blk)

    def score_block(j, _):
        kc = k6_ref[pl.ds(pl.multiple_of(j * blk, blk), blk), :]
        sc = jnp.zeros((blk, blk), F32)
        for h in range(idx_heads):
            sc = sc + wt_ref[0, h:h + 1, :] * jnp.maximum(_dot(kc, q6t_ref[0, h]), 0.0)
        key = _float_key(jnp.where(j * blk + krow <= qpos, sc, -jnp.inf))
        hi_ref[j] = lax.shift_right_arithmetic(key, 16).astype(I16)
        lo_ref[j] = (key ^ 0x8000).astype(I16)
        return 0

    lax.fori_loop(0, n_search * SEARCH_BLOCKS, score_block, 0)

    def idx16(j):
        return (j * blk + krow).astype(I16)

    def count(pred):
        def body(c, acc):
            for b in range(SEARCH_BLOCKS):
                j = c * SEARCH_BLOCKS + b
                ind = jnp.where(pred(hi_ref[j], lo_ref[j], j), jnp.ones((), I16), jnp.zeros((), I16))
                for r in range(blk // PACKED_ROWS):
                    acc = acc + ind[r * PACKED_ROWS:(r + 1) * PACKED_ROWS, :]
            return acc
        acc = lax.fori_loop(0, n_search, body, jnp.zeros((PACKED_ROWS, blk), I16))
        return jnp.sum(acc.astype(I32), axis=0, keepdims=True)

    hi_t = _search16(lambda c: count(lambda hi, lo, j: hi >= c), top, shape)
    hi_t16 = hi_t.astype(I16)
    n_above = count(lambda hi, lo, j: hi > hi_t16)
    n_band = count(lambda hi, lo, j: hi == hi_t16)

    def band_block(j, _):
        lo_ref[j] = jnp.where(hi_ref[j] == hi_t16, lo_ref[j], jnp.full((), I16_MIN, I16))
        return 0

    lax.fori_loop(0, n_search * SEARCH_BLOCKS, band_block, 0)
    lo_t = _search16(lambda c: count(lambda hi, lo, j: lo >= c), top - n_above, shape)
    lo_t16 = lo_t.astype(I16)
    n_gt = n_above + count(lambda hi, lo, j: lo > lo_t16)
    n_ge = n_above + jnp.where(lo_t == I16_MIN, n_band, count(lambda hi, lo, j: lo >= lo_t16))
    need = top - n_gt
    short = hi_t <= HI_NEG_INF
    hi_t16 = jnp.where(short, HI_NEG_INF, hi_t).astype(I16)
    lo_t16 = jnp.where(short, LO_NEG_INF, lo_t).astype(I16)
    n_ge = jnp.where(short, 0, n_ge)

    idx_bits = (seq - 1).bit_length()

    def tie_search():
        def step(s, pos):
            cand = pos + lax.shift_left(jnp.int32(1), idx_bits - 1 - s)
            c16 = cand.astype(I16)
            cnt = count(lambda hi, lo, j: (hi == hi_t16) & (lo == lo_t16) & (idx16(j) < c16))
            return jnp.where(cnt < need, cand, pos)
        return lax.fori_loop(0, idx_bits, step, jnp.zeros(shape, I32))

    pos = lax.cond(jnp.max(n_ge) > top, tie_search, lambda: jnp.full(shape, (1 << idx_bits) - 1, I32))
    pos16 = jnp.where(short, -1, pos).astype(I16)

    def mask_block(j, _):
        hi, lo = hi_ref[j], lo_ref[j]
        tie = (lo == lo_t16) & (idx16(j) <= pos16)
        sel = (hi > hi_t16) | ((hi == hi_t16) & ((lo > lo_t16) | tie))
        mask_ref[j] = jnp.where(sel, jnp.zeros((), BF16), jnp.full((), NEG_INF, BF16)).astype(F32)
        return 0

    lax.fori_loop(0, i + 1, mask_block, 0)

    sub = lax.broadcasted_iota(I32, (LANES, blk), 0)
    for h in range(heads):
        g = h // pair
        own = (sub >= (h % pair) * hd) & (sub < (h % pair + 1) * hd)
        qm_ref[h] = jnp.where(own, qt_ref[0, g * LANES:(g + 1) * LANES, :], jnp.zeros((), BF16))

    m_ref[...] = jnp.full(m_ref.shape, NEG_INF, F32)
    acc_ref[...] = jnp.zeros(acc_ref.shape, F32)

    def attend(j, near):
        rows = pl.ds(pl.multiple_of(j * blk, blk), blk)
        addm = mask_ref[j]
        for h in range(heads):
            g = h // pair
            s = _dot(k_ref[rows, g * LANES:(g + 1) * LANES], qm_ref[h]) + addm
            if near is not None:
                s = s + bias_ref[near, h]
            s_ref[h] = s
            m_old = m_ref[h]
            m_new = jnp.maximum(m_old, jnp.max(s, axis=0, keepdims=True))
            a_ref[h] = jnp.exp2(m_old - m_new)
            m_ref[h] = m_new
        for h in range(heads):
            p_ref[h] = jnp.exp2(s_ref[h] - m_ref[h]).astype(BF16)
        for h in range(heads):
            acc_ref[h] = a_ref[h] * acc_ref[h] + _dot(vt_ref[0, j, h], p_ref[h])

    def far(j, _):
        attend(j, None)
        return 0

    lax.fori_loop(0, i - 1, far, 0)

    @pl.when(i >= 1)
    def _():
        attend(i - 1, 1)

    attend(i, 0)

    for g in range(heads // pair):
        o = jnp.concatenate([acc_ref[g * pair + r, 0:hd, :] / acc_ref[g * pair + r, hd:hd + 1, :]
                             for r in range(pair)], axis=0)
        o_ref[:, g * LANES:(g + 1) * LANES] = o.T


def _dsa_prompt(qt, q6t, wt, k6, k_bf, vt, bias, nb, seq, heads, top):
    blk = DSA_BLOCK
    w = k_bf.shape[1]
    hd = w // heads
    idx_heads, kd = q6t.shape[1], q6t.shape[2]
    vrows = vt.shape[3]
    assert seq % (blk * SEARCH_BLOCKS) == 0 and LANES % hd == 0 and blk >= REL_MAX_DIST
    assert seq <= 2 ** 15 and vrows > hd
    nq = seq // blk
    once = pl.Buffered(1)
    kern = functools.partial(_dsa_prompt_kernel, heads=heads, hd=hd, idx_heads=idx_heads, top=top, seq=seq)
    return pl.pallas_call(
        kern,
        grid=(nb, nq),
        in_specs=[
            pl.BlockSpec((1, w, blk), lambda n, i: (n, 0, i)),
            pl.BlockSpec((1, idx_heads, kd, blk), lambda n, i: (n, 0, 0, i)),
            pl.BlockSpec((1, idx_heads, blk), lambda n, i: (n, 0, i)),
            pl.BlockSpec((seq, kd), lambda n, i: (n, 0), pipeline_mode=once),
            pl.BlockSpec((seq, w), lambda n, i: (n, 0), pipeline_mode=once),
            pl.BlockSpec((1, nq, heads, vrows, blk), lambda n, i: (n, 0, 0, 0, 0), pipeline_mode=once),
            pl.BlockSpec((2, heads, blk, blk), lambda n, i: (0, 0, 0, 0), pipeline_mode=once),
        ],
        out_specs=pl.BlockSpec((blk, w), lambda n, i: (n * nq + i, 0)),
        out_shape=jax.ShapeDtypeStruct((nb * seq, w), F32),
        scratch_shapes=[pltpu.VMEM((nq, blk, blk), F32),
                        pltpu.VMEM((nq, blk, blk), I16),
                        pltpu.VMEM((nq, blk, blk), I16),
                        pltpu.VMEM((heads, LANES, blk), BF16),
                        pltpu.VMEM((heads, 1, blk), F32),
                        pltpu.VMEM((heads, vrows, blk), F32),
                        pltpu.VMEM((heads, blk, blk), F32),
                        pltpu.VMEM((heads, blk, blk), BF16),
                        pltpu.VMEM((heads, 1, blk), F32)],
        compiler_params=_params("parallel", "arbitrary"),
        name="dsa_prompt",
    )(qt, q6t, wt, k6, k_bf, vt, bias)


def _dsa_sample_scores_kernel(pt_ref, q_ref, w_ref, kown_ref, *rest, n_pages, page):
    del pt_ref
    pages, o_ref = rest[:n_pages], rest[n_pages]
    q = q_ref[0]
    qh, qm, ql = _split3(q)
    w = w_ref[0]
    for p in range(n_pages):
        kh, km, kl = _split3(pages[p][0])
        d = (_dot_nt(qh, kh) + _dot_nt(qh, km) + _dot_nt(qm, kh)
             + _dot_nt(qh, kl) + _dot_nt(qm, km) + _dot_nt(ql, kh))
        sc = jnp.sum(w * jnp.maximum(d, 0.0), axis=0, keepdims=True)
        o_ref[0, :, p * page:(p + 1) * page] = sc
    d_own = jnp.sum(q * kown_ref[0], axis=-1, keepdims=True)
    s_own = jnp.sum(w * jnp.maximum(d_own, 0.0), axis=0, keepdims=True)
    lane = lax.broadcasted_iota(I32, (1, LANES), 1)
    o_ref[0, :, n_pages * page:] = jnp.where(lane == 0, s_own, -jnp.inf)


def _dsa_sample_scores(page_table, q_idx, w_idx, k_own, cache_kidx):
    nb, n_pages = page_table.shape
    page, idx_dim = cache_kidx.shape[1:]
    assert page % LANES == 0
    width = n_pages * page + LANES
    kern = functools.partial(_dsa_sample_scores_kernel, n_pages=n_pages, page=page)
    page_specs = [pl.BlockSpec((1, page, idx_dim), lambda n, pt, p=p: (pt[n * n_pages + p], 0, 0))
                  for p in range(n_pages)]
    grid_spec = pltpu.PrefetchScalarGridSpec(
        num_scalar_prefetch=1,
        grid=(nb,),
        in_specs=[pl.BlockSpec((1,) + q_idx.shape[1:], lambda n, pt: (n, 0, 0)),
                  pl.BlockSpec((1,) + w_idx.shape[1:], lambda n, pt: (n, 0, 0)),
                  pl.BlockSpec((1, 1, idx_dim), lambda n, pt: (n, 0, 0))] + page_specs,
        out_specs=pl.BlockSpec((1, 1, width), lambda n, pt: (n, 0, 0)),
    )
    return pl.pallas_call(
        kern,
        grid_spec=grid_spec,
        out_shape=jax.ShapeDtypeStruct((nb, 1, width), F32),
        compiler_params=_params("arbitrary"),
        name="dsa_sample_scores",
    )(page_table.reshape(-1), q_idx, w_idx, k_own, *([cache_kidx] * n_pages))


def _dsa_sample_mask_kernel(s_ref, o_ref, *, top, width):
    keys = _float_key(s_ref[...])
    idx = lax.broadcasted_iota(I32, keys.shape, 1)

    def count(pred):
        return jnp.sum(jnp.where(pred(keys, idx), 1.0, 0.0), axis=-1, keepdims=True)

    shape = (keys.shape[0], 1)
    thr, n_ge, need = _topk_select(count, shape, top)
    pos = _tie_bound(count, shape, thr, need, n_ge, top, (width - 1).bit_length())
    sel = (keys > thr) | ((keys == thr) & (idx <= pos))
    o_ref[...] = jnp.where(sel, 0.0, NEG_INF)


def _dsa_sample_mask(scores, top):
    rows, width = scores.shape
    return pl.pallas_call(
        functools.partial(_dsa_sample_mask_kernel, top=top, width=width),
        out_shape=jax.ShapeDtypeStruct((rows, width), F32),
        compiler_params=pltpu.CompilerParams(vmem_limit_bytes=VMEM_LIMIT),
        name="dsa_sample_mask",
    )(scores)


def _dsa_sample_attend_kernel(pt_ref, q_ref, kown_ref, vown_ref, mask_ref, mown_ref, bias_ref, bown_ref,
                              *rest, n_pages, page, heads, hd):
    del pt_ref
    kp, vp, o_ref = rest[:n_pages], rest[n_pages:2 * n_pages], rest[2 * n_pages]
    pw = page * heads
    q = q_ref[0] * (hd ** -0.5)
    q_bf = q.astype(BF16)
    logits = []
    for p in range(n_pages):
        sl = slice(p * pw, (p + 1) * pw)
        kmat = kp[p][0].reshape(pw, hd).astype(BF16)
        logits.append(_dot_nt(q_bf, kmat) + bias_ref[:, sl] + mask_ref[0, :, sl])
    s_own = jnp.sum(q * kown_ref[0], axis=-1, keepdims=True) + bown_ref[...] + mown_ref[0]
    m = s_own
    for s in logits:
        m = jnp.maximum(m, jnp.max(s, axis=-1, keepdims=True))
    p_own = jnp.exp(s_own - m)
    l = p_own
    acc = p_own * vown_ref[0]
    for p in range(n_pages):
        pr = jnp.exp(logits[p] - m)
        l = l + jnp.sum(pr, axis=-1, keepdims=True)
        acc = acc + _dot(pr.astype(BF16), vp[p][0].reshape(pw, hd).astype(BF16))
    o_ref[0] = acc / l


def _dsa_sample_attend(page_table, q, k_own, v_own, mask, mask_own, bias, bias_own, cache_k, cache_v):
    nb, n_pages = page_table.shape
    _, page, heads, hd = cache_k.shape
    assert heads == SUBLANES
    kern = functools.partial(_dsa_sample_attend_kernel, n_pages=n_pages, page=page, heads=heads, hd=hd)
    page_specs = [pl.BlockSpec((1, page, heads, hd), lambda n, pt, p=p: (pt[n * n_pages + p], 0, 0, 0))
                  for p in range(n_pages)]
    row = pl.BlockSpec((1, heads, hd), lambda n, pt: (n, 0, 0))
    grid_spec = pltpu.PrefetchScalarGridSpec(
        num_scalar_prefetch=1,
        grid=(nb,),
        in_specs=[row, row, row,
                  pl.BlockSpec((1, 1, mask.shape[-1]), lambda n, pt: (n, 0, 0)),
                  pl.BlockSpec((1, 1, 1), lambda n, pt: (n, 0, 0)),
                  pl.BlockSpec(bias.shape, lambda n, pt: (0, 0)),
                  pl.BlockSpec(bias_own.shape, lambda n, pt: (0, 0))] + page_specs + page_specs,
        out_specs=row,
    )
    return pl.pallas_call(
        kern,
        grid_spec=grid_spec,
        out_shape=jax.ShapeDtypeStruct((nb, heads, hd), F32),
        compiler_params=_params("arbitrary"),
        name="dsa_sample_attend",
    )(page_table.reshape(-1), q, k_own, v_own, mask, mask_own, bias, bias_own,
      *([cache_k] * n_pages), *([cache_v] * n_pages))


def _merge_kernel(x_ref, y_ref, a_ref, m_ref, g1_ref, g2_ref, g3_ref, wc_ref, wa_ref, wm_ref, wo_ref,
                  lg_ref, lb_ref, o_ref):
    c = _dot(y_ref[...].astype(BF16), wc_ref[...])
    a = _dot(a_ref[...].astype(BF16), wa_ref[...])
    m = _dot(m_ref[...].astype(BF16), wm_ref[...])
    merged = (jax.nn.sigmoid(g1_ref[...]) * c + jax.nn.sigmoid(g2_ref[...]) * a
              + jax.nn.sigmoid(g3_ref[...]) * m)
    h = _dot(merged.astype(BF16), wo_ref[...])
    o_ref[...] = _layer_norm(ALPHA * x_ref[...] + h, lg_ref[...], lb_ref[...])


def _merge(x, y_act, att, mem, z_main, g_col, wc, wa, wm, wo, ln_g, ln_b):
    n, d = x.shape
    tm = _pick_tile(n, (256, 128, 64, 32, 16, 8))
    row = lambda width: pl.BlockSpec((tm, width), lambda i: (i, 0))
    gate = lambda b: pl.BlockSpec((tm, d), lambda i: (i, g_col + b))
    full = lambda a: pl.BlockSpec(a.shape, lambda i: (0, 0))
    return pl.pallas_call(
        _merge_kernel,
        grid=(n // tm,),
        in_specs=[row(d), row(y_act.shape[1]), row(att.shape[1]), row(mem.shape[1]),
                  gate(0), gate(1), gate(2), full(wc), full(wa), full(wm), full(wo), full(ln_g), full(ln_b)],
        out_specs=row(d),
        out_shape=jax.ShapeDtypeStruct((n, d), F32),
        compiler_params=_params("parallel"),
        name="merge_ln1",
    )(x, y_act, att, mem, z_main, z_main, z_main, wc, wa, wm, wo, ln_g, ln_b)


MOE_BLOCK = 128
MOE_TOKENS = 128


def _router_kernel(x_ref, wh_ref, wl_ref, rb_ref, e_ref, g_ref, r_ref, cnt_ref, carry_ref, *, topk):
    i = pl.program_id(0)

    @pl.when(i == 0)
    def _():
        carry_ref[...] = jnp.zeros(carry_ref.shape, F32)

    xh, xl = _split2(x_ref[...])
    logits = _dot(xh, wh_ref[...]) + _dot(xl, wh_ref[...]) + _dot(xh, wl_ref[...])
    scores = jax.nn.sigmoid(logits)
    tm, ne = scores.shape
    sel = scores + rb_ref[...]
    lane = lax.broadcasted_iota(I32, (tm, ne), 1).astype(F32)
    chosen = jnp.zeros((tm, ne), F32)
    picks = []
    total = jnp.zeros((tm, 1), F32)
    for _ in range(topk):
        mx = jnp.max(sel, axis=-1, keepdims=True)
        idx = jnp.min(jnp.where(sel == mx, lane, float(ne)), axis=-1, keepdims=True)
        hot = lane == idx
        sk = jnp.sum(jnp.where(hot, scores, 0.0), axis=-1, keepdims=True)
        sel = jnp.where(hot, -jnp.inf, sel)
        chosen = jnp.where(hot, 1.0, chosen)
        total = total + sk
        picks.append((idx, sk, hot))
    r = lax.broadcasted_iota(I32, (tm, tm), 0)
    c = lax.broadcasted_iota(I32, (tm, tm), 1)
    before = jnp.where(c < r, 1.0, 0.0).astype(BF16)
    rank = _dot(before, chosen.astype(BF16)) + carry_ref[...]
    out_lane = lax.broadcasted_iota(I32, (tm, LANES), 1)
    e_out = jnp.zeros((tm, LANES), I32)
    g_out = jnp.zeros((tm, LANES), F32)
    r_out = jnp.zeros((tm, LANES), I32)
    for k, (idx, sk, hot) in enumerate(picks):
        rk = jnp.sum(jnp.where(hot, rank, 0.0), axis=-1, keepdims=True)
        e_out = jnp.where(out_lane == k, idx.astype(I32), e_out)
        g_out = jnp.where(out_lane == k, sk / total * ROUTED_SCALE, g_out)
        r_out = jnp.where(out_lane == k, rk.astype(I32), r_out)
    e_ref[...] = e_out
    g_ref[...] = g_out
    r_ref[...] = r_out
    carry_ref[...] = carry_ref[...] + jnp.sum(chosen, axis=0, keepdims=True)
    cnt_ref[...] = carry_ref[...]


def _router(x1, w_router, router_bias):
    n, d = x1.shape
    ne = w_router.shape[1]
    tm = _pick_tile(n, (256, 128, 64, 48, 32, 16, 8))
    wh, wl = _split2(w_router)
    out = lambda: pl.BlockSpec((tm, LANES), lambda i: (i, 0))
    return pl.pallas_call(
        functools.partial(_router_kernel, topk=TOP_K),
        grid=(n // tm,),
        in_specs=[pl.BlockSpec((tm, d), lambda i: (i, 0)),
                  pl.BlockSpec((d, ne), lambda i: (0, 0)),
                  pl.BlockSpec((d, ne), lambda i: (0, 0)),
                  pl.BlockSpec((1, ne), lambda i: (0, 0))],
        out_specs=[out(), out(), out(), pl.BlockSpec((1, ne), lambda i: (0, 0))],
        out_shape=[jax.ShapeDtypeStruct((n, LANES), I32), jax.ShapeDtypeStruct((n, LANES), F32),
                   jax.ShapeDtypeStruct((n, LANES), I32), jax.ShapeDtypeStruct((1, ne), F32)],
        scratch_shapes=[pltpu.VMEM((1, ne), F32)],
        compiler_params=_params("arbitrary"),
        name="router",
    )(x1, wh, wl, router_bias)


def _dispatch_kernel(dest_ref, x_ref, init_ref, xs_ref, sem, *, tt, topk):
    del init_ref

    def copy(t, d):
        return pltpu.make_async_copy(x_ref.at[pl.ds(t, 1)], xs_ref.at[pl.ds(d, 1)], sem)

    def start(t, _):
        for k in range(topk):
            copy(t, dest_ref[t * topk + k]).start()
        return 0

    lax.fori_loop(0, tt, start, 0)

    def wait(t, _):
        for k in range(topk):
            copy(t, dest_ref[t * topk + k]).wait()
        return 0

    lax.fori_loop(0, tt, wait, 0)


def _dispatch(x1, dest, cap):
    n, d = x1.shape
    topk = dest.shape[1]
    tt = _pick_tile(n, (MOE_TOKENS, 64, 48, 32, 16, 8))
    init = jnp.zeros((cap, d), F32)
    return pl.pallas_call(
        functools.partial(_dispatch_kernel, tt=tt, topk=topk),
        grid=(n // tt,),
        in_specs=[pl.BlockSpec((tt * topk,), lambda i: (i,), memory_space=pltpu.SMEM),
                  pl.BlockSpec((tt, d), lambda i: (i, 0)),
                  pl.BlockSpec(memory_space=pl.ANY)],
        out_specs=pl.BlockSpec(memory_space=pl.ANY),
        out_shape=jax.ShapeDtypeStruct((cap, d), F32),
        scratch_shapes=[pltpu.SemaphoreType.DMA(())],
        input_output_aliases={2: 0},
        compiler_params=_params("arbitrary"),
        name="moe_dispatch",
    )(dest.reshape(-1), x1, init)


def _expert_kernel(be_ref, na_ref, xs_ref, wgu_ref, wdn_ref, y_ref, wgu_bf, wdn_bf, *, de):
    b = pl.program_id(0)

    @pl.when(b < na_ref[0])
    def _():
        @pl.when((b == 0) | (be_ref[b] != be_ref[jnp.maximum(b - 1, 0)]))
        def _():
            wgu_bf[...] = wgu_ref[0].astype(BF16)
            wdn_bf[...] = wdn_ref[0].astype(BF16)

        h = _dot(xs_ref[...].astype(BF16), wgu_bf[...])
        act = _silu(h[:, :de]) * h[:, de:]
        y_ref[...] = _dot(act.astype(BF16), wdn_bf[...])

    @pl.when(b >= na_ref[0])
    def _():
        y_ref[...] = jnp.zeros(y_ref.shape, F32)


def _experts(xs, blk_exp, n_active, w_gu, w_down):
    cap, d = xs.shape
    ne, _, de2 = w_gu.shape
    de = de2 // 2
    n_blk = cap // MOE_BLOCK
    grid_spec = pltpu.PrefetchScalarGridSpec(
        num_scalar_prefetch=2,
        grid=(n_blk,),
        in_specs=[pl.BlockSpec((MOE_BLOCK, d), lambda b, be, na: (b, 0)),
                  pl.BlockSpec((1, d, de2), lambda b, be, na: (be[b], 0, 0)),
                  pl.BlockSpec((1, de, d), lambda b, be, na: (be[b], 0, 0))],
        out_specs=pl.BlockSpec((MOE_BLOCK, d), lambda b, be, na: (b, 0)),
        scratch_shapes=[pltpu.VMEM((d, de2), BF16), pltpu.VMEM((de, d), BF16)],
    )
    return pl.pallas_call(
        functools.partial(_expert_kernel, de=de),
        grid_spec=grid_spec,
        out_shape=jax.ShapeDtypeStruct((cap, d), F32),
        compiler_params=_params("arbitrary"),
        name="moe_experts",
    )(blk_exp, n_active, xs, w_gu, w_down)


def _combine_kernel(dest_ref, gate_ref, x_ref, y_ref, wgu_ref, wdn_ref, lg_ref, lb_ref, o_ref, buf, sem,
                    *, tt, topk, ds):
    def copy(t, k):
        return pltpu.make_async_copy(y_ref.at[pl.ds(dest_ref[t * topk + k], 1)], buf.at[k, pl.ds(t, 1)], sem)

    def start(t, _):
        for k in range(topk):
            copy(t, k).start()
        return 0

    lax.fori_loop(0, tt, start, 0)

    x = x_ref[...]
    h = _dot(x.astype(BF16), wgu_ref[...])
    shared = _dot((_silu(h[:, :ds]) * h[:, ds:]).astype(BF16), wdn_ref[...])

    def wait(t, _):
        for k in range(topk):
            copy(t, k).wait()
        return 0

    lax.fori_loop(0, tt, wait, 0)

    routed = jnp.zeros(x.shape, F32)
    for k in range(topk):
        routed = routed + gate_ref[:, k:k + 1] * buf[k]
    o_ref[...] = _layer_norm(ALPHA * x + routed + shared, lg_ref[...], lb_ref[...])


def _combine(x1, y_rows, dest, gates, w_sh_gu, w_sh_down, ln_g, ln_b):
    n, d = x1.shape
    topk = dest.shape[1]
    tt = _pick_tile(n, (MOE_TOKENS, 64, 48, 32, 16, 8))
    full = lambda a: pl.BlockSpec(a.shape, lambda i: (0, 0))
    return pl.pallas_call(
        functools.partial(_combine_kernel, tt=tt, topk=topk, ds=w_sh_down.shape[0]),
        grid=(n // tt,),
        in_specs=[pl.BlockSpec((tt * topk,), lambda i: (i,), memory_space=pltpu.SMEM),
                  pl.BlockSpec((tt, LANES), lambda i: (i, 0)),
                  pl.BlockSpec((tt, d), lambda i: (i, 0)),
                  pl.BlockSpec(memory_space=pl.ANY),
                  full(w_sh_gu), full(w_sh_down), full(ln_g), full(ln_b)],
        out_specs=pl.BlockSpec((tt, d), lambda i: (i, 0)),
        out_shape=jax.ShapeDtypeStruct((n, d), F32),
        scratch_shapes=[pltpu.VMEM((topk, tt, d), F32), pltpu.SemaphoreType.DMA(())],
        compiler_params=_params("arbitrary"),
        name="moe_combine",
    )(dest.reshape(-1), gates, x1, y_rows, w_sh_gu, w_sh_down, ln_g, ln_b)


def _moe_ln2(x1, w_router, router_bias, w_exp_gu, w_exp_down, w_sh_gu, w_sh_down, ln_g, ln_b):
    n, d = x1.shape
    ne = w_router.shape[1]
    top_e, gates, rank, counts = _router(x1, w_router, router_bias)
    counts = counts[0].astype(I32)
    padded = (counts + MOE_BLOCK - 1) // MOE_BLOCK * MOE_BLOCK
    pend = jnp.cumsum(padded)
    pstart = pend - padded
    hit = top_e[:, :TOP_K, None] == jnp.arange(ne, dtype=I32)
    dest = jnp.sum(jnp.where(hit, pstart, 0), axis=-1) + rank[:, :TOP_K]
    cap = (n * TOP_K + ne * (MOE_BLOCK - 1) + MOE_BLOCK - 1) // MOE_BLOCK * MOE_BLOCK
    n_blk = cap // MOE_BLOCK
    blk_exp = jnp.minimum(jnp.searchsorted(pend, jnp.arange(n_blk, dtype=I32) * MOE_BLOCK, side='right'),
                          ne - 1).astype(I32)
    n_active = (pend[-1:] // MOE_BLOCK).astype(I32)
    xs = _dispatch(x1, dest, cap)
    y_rows = _experts(xs, blk_exp, n_active, w_exp_gu, w_exp_down)
    return _combine(x1, y_rows, dest, gates, w_sh_gu.astype(BF16), w_sh_down.astype(BF16), ln_g, ln_b)


def _rel_bucket(dist):
    n = jnp.maximum(dist, 0)
    max_exact = N_BUCKETS // 2
    nf = jnp.maximum(n, max_exact).astype(F32)
    large = max_exact + (jnp.log(nf / max_exact) / math.log(REL_MAX_DIST / max_exact)
                         * (N_BUCKETS - max_exact)).astype(I32)
    return jnp.where(n < max_exact, n, jnp.minimum(large, N_BUCKETS - 1))


def _bias_table(rel_bias, dist):
    hit = _rel_bucket(dist)[None, :, None] == jnp.arange(N_BUCKETS, dtype=I32)[None, None, :]
    return jnp.sum(jnp.where(hit, rel_bias.T[:, None, :], 0.0), axis=-1)


def kernel(x_prompt, x_sample, mem_prompt, cache_k, cache_v, cache_kidx, cache_mem_k, cache_mem_v, state_conv, page_table, rel_bias, w_in, b_in, conv_w, conv_b, conv_ln_g, conv_ln_b, w_conv_out, w_attn_out, w_mem_kv, w_mem_out, w_out, ln1_g, ln1_b, w_router, router_bias, w_exp_gu, w_exp_down, w_sh_gu, w_sh_down, ln2_g, ln2_b):
    assert w_in.shape[0] == DEPTH == 1
    nb, seq, d = x_prompt.shape
    nsb, nst, _ = x_sample.shape
    assert nst == 1
    width, dc = conv_w.shape[1:]
    _, n_pool, page, heads, hd = cache_k.shape
    idx_dim = cache_kidx.shape[-1]
    n_mem, mem_heads, mem_hd = cache_mem_k.shape[2:]
    aw, mw, iw = heads * hd, mem_heads * mem_hd, N_IDX_HEADS * idx_dim
    assert 2 * dc == d and aw * 2 == d and mw * 2 == d

    o_q = 2 * dc
    o_qi = o_q + 3 * aw
    o_ki = o_qi + iw
    o_wi = o_ki + idx_dim
    o_qm = o_wi + N_IDX_HEADS
    o_g = o_qm + mw
    w0, b0 = w_in[0], b_in[0]
    w_main = jnp.concatenate([w0[:, :o_qi], w0[:, o_qm:]], axis=1).astype(BF16)
    b_main = jnp.concatenate([b0[:o_qi], b0[o_qm:]])[None, :]
    n_idx = o_qm - o_qi
    n_idx_pad = -(-n_idx // LANES) * LANES
    w_idx = jnp.pad(w0[:, o_qi:o_qm], ((0, 0), (0, n_idx_pad - n_idx)))
    b_idx = jnp.pad(b0[o_qi:o_qm], (0, n_idx_pad - n_idx))[None, :]
    col_q, col_k, col_v, col_qm, col_g = 2, 3, 4, 5, 3

    xp = x_prompt.reshape(nb * seq, d)
    xs = x_sample.reshape(nsb, d)
    zp = _linear(xp, w_main, b_main)
    zs = _linear(xs, w_main, b_main)
    zip_ = _linear(xp, w_idx, b_idx, split=True, tn=n_idx_pad)
    zis = _linear(xs, w_idx, b_idx, split=True, tn=n_idx_pad)

    kp, vp = zp[:, col_k * aw:(col_k + 1) * aw], zp[:, col_v * aw:(col_v + 1) * aw]
    ks, vs = zs[:, col_k * aw:(col_k + 1) * aw], zs[:, col_v * aw:(col_v + 1) * aw]
    kip, kis = zip_[:, iw:iw + idx_dim], zis[:, iw:iw + idx_dim]

    conv0 = jnp.zeros((nb, width - 1, dc), F32)
    cw, cb, cg, cbb = conv_w[0], conv_b[0][None, :], conv_ln_g[0][None, :], conv_ln_b[0][None, :]
    yp, tail_p = _conv_prompt(zp, conv0, cw, cb, cg, cbb, nb, seq)
    conv_p = tail_p[:, CONV_HALO - (width - 1):, :]
    ys, new_t = _conv_sample(zs, jnp.swapaxes(state_conv[0], 0, 1), cw, cb, cg, cbb)
    conv_s = jnp.swapaxes(new_t, 0, 1)

    wkv = w_mem_kv[0].astype(BF16)
    mkv = _linear(mem_prompt.reshape(nb * n_mem, d), wkv, jnp.zeros((1, 2 * mw), F32))
    mk_p, mv_p = mkv[:, :mw], mkv[:, mw:]
    mem_p = _mem_prompt(zp, col_qm, mk_p, mv_p, nb, seq, mem_heads)
    mem_s = _mem_sample(zs, col_qm, cache_mem_k[0].reshape(nsb, n_mem, mw),
                        cache_mem_v[0].reshape(nsb, n_mem, mw), mem_heads)

    far = _bias_table(rel_bias, jnp.full((1,), 4 * REL_MAX_DIST, I32))
    a = jnp.arange(DSA_BLOCK, dtype=I32)
    dist = jnp.stack([a[None, :] - a[:, None], DSA_BLOCK + a[None, :] - a[:, None]])
    bias_p = (_bias_table(rel_bias, dist.reshape(-1)) - far).reshape(heads, 2, DSA_BLOCK, DSA_BLOCK)
    bias_p = jnp.swapaxes(bias_p, 0, 1) * LOG2E
    qh, qm_, ql = _split3(zip_[:, :iw].reshape(nb, seq, N_IDX_HEADS, idx_dim))
    q6t = jnp.transpose(jnp.concatenate([qh, qh, qm_, qh, qm_, ql], axis=-1), (0, 2, 3, 1))
    kh, km, kl = _split3(kip)
    k6 = jnp.concatenate([kh, km, kh, kl, km, kh], axis=-1)
    wt = jnp.swapaxes(zip_[:, iw + idx_dim:iw + idx_dim + N_IDX_HEADS].reshape(nb, seq, N_IDX_HEADS), 1, 2)
    qt = jnp.swapaxes((zp[:, col_q * aw:(col_q + 1) * aw] * (hd ** -0.5 * LOG2E)).astype(BF16)
                      .reshape(nb, seq, aw), 1, 2)
    nkb = seq // DSA_BLOCK
    vt = jnp.transpose(vp.astype(BF16).reshape(nb, nkb, DSA_BLOCK, heads, hd), (0, 1, 3, 4, 2))
    vt = jnp.concatenate([vt, jnp.ones((nb, nkb, heads, 1, DSA_BLOCK), BF16),
                          jnp.zeros((nb, nkb, heads, PACKED_ROWS - 1, DSA_BLOCK), BF16)], axis=3)
    top_p = min(TOPK_MAX, seq // 4)
    att_p = _dsa_prompt(qt, q6t, wt, k6, kp.astype(BF16), vt, bias_p, nb, seq, heads, top_p)

    n_pages = page_table.shape[1]
    past = n_pages * page
    top_s = min(TOPK_MAX, (past + 1) // 4)
    scores = _dsa_sample_scores(page_table, zis[:, :iw].reshape(nsb, N_IDX_HEADS, idx_dim),
                                zis[:, iw + idx_dim:iw + idx_dim + N_IDX_HEADS].reshape(nsb, N_IDX_HEADS, 1),
                                kis.reshape(nsb, 1, idx_dim), cache_kidx[0])
    mask_s = _dsa_sample_mask(scores.reshape(nsb, past + LANES), top_s)
    spos = jnp.arange(past + LANES, dtype=I32)
    bias_s = _bias_table(rel_bias, past - spos)
    mask_x = jnp.repeat(mask_s[:, :past], heads, axis=1).reshape(nsb, 1, past * heads)
    own_head = jnp.arange(heads, dtype=I32)[:, None, None] == jnp.arange(heads, dtype=I32)[None, None, :]
    bias_x = jnp.where(own_head, bias_s[:, :past, None], NEG_INF).reshape(heads, past * heads)
    att_s = _dsa_sample_attend(page_table, zs[:, col_q * aw:(col_q + 1) * aw].reshape(nsb, heads, hd),
                               ks.reshape(nsb, heads, hd), vs.reshape(nsb, heads, hd),
                               mask_x, mask_s[:, past:past + 1].reshape(nsb, 1, 1),
                               bias_x, bias_s[:, past:past + 1],
                               cache_k[0], cache_v[0]).reshape(nsb, aw)

    wc, wa, wm, wo = (w_conv_out[0].astype(BF16), w_attn_out[0].astype(BF16),
                      w_mem_out[0].astype(BF16), w_out[0].astype(BF16))
    l1g, l1b = ln1_g[0][None, :], ln1_b[0][None, :]
    x1p = _merge(xp, yp, att_p, mem_p, zp, col_g, wc, wa, wm, wo, l1g, l1b)
    x1s = _merge(xs, ys, att_s, mem_s, zs, col_g, wc, wa, wm, wo, l1g, l1b)
    x1 = jnp.concatenate([x1p, x1s], axis=0)
    x2 = _moe_ln2(x1, w_router[0], router_bias[0][None, :], w_exp_gu[0], w_exp_down[0],
                  w_sh_gu[0], w_sh_down[0], ln2_g[0][None, :], ln2_b[0][None, :])

    return (x2[:nb * seq].reshape(nb, seq, d), x2[nb * seq:].reshape(nsb, 1, d),
            kp.reshape(1, nb, seq, heads, hd), vp.reshape(1, nb, seq, heads, hd),
            kip.reshape(1, nb, seq, idx_dim),
            mk_p.reshape(1, nb, n_mem, mem_heads, mem_hd), mv_p.reshape(1, nb, n_mem, mem_heads, mem_hd),
            conv_p[None],
            ks.reshape(1, nsb, 1, heads, hd), vs.reshape(1, nsb, 1, heads, hd),
            kis.reshape(1, nsb, 1, idx_dim), conv_s[None])
```

```python
import functools
import math

import jax
import jax.numpy as jnp
from jax import lax
from jax.experimental import pallas as pl
from jax.experimental.pallas import tpu as pltpu

F32 = jnp.float32
BF16 = jnp.bfloat16
I32 = jnp.int32

DEPTH = 1
N_IDX_HEADS = 8
TOPK_MAX = 256
N_BUCKETS = 32
REL_MAX_DIST = 128
TOP_K = 8
ROUTED_SCALE = 2.5
ALPHA = (2 * DEPTH) ** 0.25
LN_EPS = 1e-5
NEG_INF = -1e30
LOG2E = math.log2(math.e)

LANES = 128
SUBLANES = 8
VMEM_LIMIT = 56 * 1024 * 1024

INT_MIN = -(2 ** 31)
KEY_NEG_INF = -2139095041

_dot = functools.partial(jnp.dot, preferred_element_type=F32)


def _dot_nt(a, b):
    return lax.dot_general(a, b, (((1,), (1,)), ((), ())), preferred_element_type=F32)


def _pick_tile(n, cands=(1024, 512, 384, 256, 128, 64, 48, 32, 16, 8)):
    for c in cands:
        if n % c == 0:
            return c
    raise ValueError(f"no row tile divides {n}")


def _params(*sem):
    return pltpu.CompilerParams(dimension_semantics=sem, vmem_limit_bytes=VMEM_LIMIT)


def _trunc_bf16(x):
    bits = lax.bitcast_convert_type(x, jnp.uint32) & jnp.uint32(0xFFFF0000)
    return lax.bitcast_convert_type(bits, F32)


def _split2(x):
    hi = _trunc_bf16(x)
    return hi.astype(BF16), (x - hi).astype(BF16)


def _split3(x):
    hi = _trunc_bf16(x)
    r = x - hi
    mid = _trunc_bf16(r)
    lo = r - mid
    return hi.astype(BF16), mid.astype(BF16), lo.astype(BF16)


def _layer_norm(x, g, b):
    mu = jnp.mean(x, axis=-1, keepdims=True)
    xc = x - mu
    var = jnp.mean(xc * xc, axis=-1, keepdims=True)
    return xc * lax.rsqrt(var + LN_EPS) * g + b


def _silu(x):
    return x * jax.nn.sigmoid(x)


def _float_key(x):
    x = jnp.where(x == 0.0, 0.0, x)
    bits = lax.bitcast_convert_type(x, I32)
    return jnp.where(bits >= 0, bits, bits ^ jnp.int32(0x7FFFFFFF))


def _linear_kernel(x_ref, w_ref, b_ref, o_ref, *, split):
    x = x_ref[...]
    if split:
        xh, xl = _split2(x)
        wh, wl = _split2(w_ref[...])
        acc = _dot(xh, wh) + _dot(xl, wh) + _dot(xh, wl)
    else:
        acc = _dot(x.astype(BF16), w_ref[...])
    o_ref[...] = acc + b_ref[...]


def _linear(x, w, b, *, split=False, tn=512):
    m, k = x.shape
    n = w.shape[1]
    tm = _pick_tile(m)
    tn = min(tn, n)
    assert n % tn == 0
    return pl.pallas_call(
        functools.partial(_linear_kernel, split=split),
        grid=(m // tm, n // tn),
        in_specs=[pl.BlockSpec((tm, k), lambda i, j: (i, 0)),
                  pl.BlockSpec((k, tn), lambda i, j: (0, j)),
                  pl.BlockSpec((1, tn), lambda i, j: (0, j))],
        out_specs=pl.BlockSpec((tm, tn), lambda i, j: (i, j)),
        out_shape=jax.ShapeDtypeStruct((m, n), F32),
        compiler_params=_params("parallel", "parallel"),
        name="linear_split" if split else "linear",
    )(x, w, b)


CONV_HALO = 32
CONV_SUB = 64


def _glu(z, dc):
    return z[:, :dc] * jax.nn.sigmoid(z[:, dc:])


def _conv_prompt_kernel(glu_ref, prev_ref, buf_ref, cw_ref, cb_ref, g_ref, b_ref, y_ref, tail_ref, ext_ref,
                        *, ts, width, dc):
    i = pl.program_id(1)
    u = _glu(glu_ref[...], dc)
    ext_ref[0:CONV_HALO, :] = _glu(prev_ref[...], dc)

    @pl.when(i == 0)
    def _():
        ext_ref[CONV_HALO - (width - 1):CONV_HALO, :] = buf_ref[0]

    ext_ref[CONV_HALO:CONV_HALO + ts, :] = u
    tail_ref[0] = u[ts - CONV_HALO:, :]
    off = CONV_HALO - (width - 1)
    for r in range(ts // CONV_SUB):
        acc = jnp.broadcast_to(cb_ref[...], (CONV_SUB, dc))
        for j in range(width):
            lo = r * CONV_SUB + off + j
            acc = acc + ext_ref[lo:lo + CONV_SUB, :] * cw_ref[j:j + 1, :]
        y_ref[r * CONV_SUB:(r + 1) * CONV_SUB, :] = _silu(_layer_norm(acc, g_ref[...], b_ref[...]))


def _conv_prompt(z_main, conv_buf, conv_w, conv_b, ln_g, ln_b, nb, seq):
    width, dc = conv_w.shape
    assert width - 1 <= CONV_HALO
    ts = 256
    assert seq % ts == 0 and ts % CONV_SUB == 0 and ts % CONV_HALO == 0
    nt = seq // ts
    kern = functools.partial(_conv_prompt_kernel, ts=ts, width=width, dc=dc)
    return pl.pallas_call(
        kern,
        grid=(nb, nt),
        in_specs=[
            pl.BlockSpec((ts, 2 * dc), lambda n, i: (n * nt + i, 0)),
            pl.BlockSpec((CONV_HALO, 2 * dc),
                         lambda n, i: (jnp.maximum((n * nt + i) * (ts // CONV_HALO) - 1, 0), 0)),
            pl.BlockSpec((1, width - 1, dc), lambda n, i: (n, 0, 0)),
            pl.BlockSpec((width, dc), lambda n, i: (0, 0)),
            pl.BlockSpec((1, dc), lambda n, i: (0, 0)),
            pl.BlockSpec((1, dc), lambda n, i: (0, 0)),
            pl.BlockSpec((1, dc), lambda n, i: (0, 0)),
        ],
        out_specs=[pl.BlockSpec((ts, dc), lambda n, i: (n * nt + i, 0)),
                   pl.BlockSpec((1, CONV_HALO, dc), lambda n, i: (n, 0, 0))],
        out_shape=[jax.ShapeDtypeStruct((nb * seq, dc), F32),
                   jax.ShapeDtypeStruct((nb, CONV_HALO, dc), F32)],
        scratch_shapes=[pltpu.VMEM((CONV_HALO + ts, dc), F32)],
        compiler_params=_params("parallel", "arbitrary"),
        name="conv_prompt",
    )(z_main, z_main, conv_buf, conv_w, conv_b, ln_g, ln_b)


def _conv_sample_kernel(glu_ref, buf_ref, cw_ref, cb_ref, g_ref, b_ref, y_ref, new_ref, *, width, dc):
    u = _glu(glu_ref[...], dc)
    acc = cb_ref[...] + u * cw_ref[width - 1:width, :]
    for j in range(width - 1):
        row = buf_ref[j]
        acc = acc + row * cw_ref[j:j + 1, :]
        if j >= 1:
            new_ref[j - 1] = row
    new_ref[width - 2] = u
    y_ref[...] = _silu(_layer_norm(acc, g_ref[...], b_ref[...]))


def _conv_sample(z_main, buf_t, conv_w, conv_b, ln_g, ln_b):
    width, dc = conv_w.shape
    nb = z_main.shape[0]
    tb = _pick_tile(nb, (32, 16, 8))
    kern = functools.partial(_conv_sample_kernel, width=width, dc=dc)
    return pl.pallas_call(
        kern,
        grid=(nb // tb,),
        in_specs=[pl.BlockSpec((tb, 2 * dc), lambda i: (i, 0)),
                  pl.BlockSpec((width - 1, tb, dc), lambda i: (0, i, 0)),
                  pl.BlockSpec((width, dc), lambda i: (0, 0)),
                  pl.BlockSpec((1, dc), lambda i: (0, 0)),
                  pl.BlockSpec((1, dc), lambda i: (0, 0)),
                  pl.BlockSpec((1, dc), lambda i: (0, 0))],
        out_specs=[pl.BlockSpec((tb, dc), lambda i: (i, 0)),
                   pl.BlockSpec((width - 1, tb, dc), lambda i: (0, i, 0))],
        out_shape=[jax.ShapeDtypeStruct((nb, dc), F32),
                   jax.ShapeDtypeStruct((width - 1, nb, dc), F32)],
        compiler_params=_params("parallel"),
        name="conv_sample",
    )(z_main, buf_t, conv_w, conv_b, ln_g, ln_b)


def _mem_prompt_kernel(q_ref, mk_ref, mv_ref, o_ref, *, heads, hd):
    scale = hd ** -0.5
    for h in range(heads):
        sl = slice(h * hd, (h + 1) * hd)
        q = (q_ref[:, sl] * scale).astype(BF16)
        s = _dot_nt(q, mk_ref[:, sl].astype(BF16))
        m = jnp.max(s, axis=-1, keepdims=True)
        p = jnp.exp(s - m)
        l = jnp.sum(p, axis=-1, keepdims=True)
        o_ref[:, sl] = _dot(p.astype(BF16), mv_ref[:, sl].astype(BF16)) / l


def _mem_prompt(z_main, q_col, mk, mv, nb, seq, heads):
    w = mk.shape[1]
    n_mem = mk.shape[0] // nb
    tq = _pick_tile(seq, (512, 256, 128))
    nt = seq // tq
    kern = functools.partial(_mem_prompt_kernel, heads=heads, hd=w // heads)
    return pl.pallas_call(
        kern,
        grid=(nb, nt),
        in_specs=[pl.BlockSpec((tq, w), lambda n, i: (n * nt + i, q_col)),
                  pl.BlockSpec((n_mem, w), lambda n, i: (n, 0)),
                  pl.BlockSpec((n_mem, w), lambda n, i: (n, 0))],
        out_specs=pl.BlockSpec((tq, w), lambda n, i: (n * nt + i, 0)),
        out_shape=jax.ShapeDtypeStruct((nb * seq, w), F32),
        compiler_params=_params("parallel", "parallel"),
        name="mem_prompt",
    )(z_main, mk, mv)


MEM_GROUP = 8


def _head_mask(heads, hd, rows=SUBLANES):
    r = lax.broadcasted_iota(I32, (rows, heads * hd), 0)
    c = lax.broadcasted_iota(I32, (rows, heads * hd), 1)
    return (c >= r * hd) & (c < (r + 1) * hd)


def _mem_sample_kernel(q_ref, mk_ref, mv_ref, o_ref, *, heads, hd):
    scale = hd ** -0.5
    hm = _head_mask(heads, hd)
    for g in range(MEM_GROUP):
        q = q_ref[g:g + 1, :] * scale
        q_bd = jnp.where(hm, jnp.broadcast_to(q, hm.shape), 0.0).astype(BF16)
        s = _dot_nt(q_bd, mk_ref[g].astype(BF16))
        m = jnp.max(s, axis=-1, keepdims=True)
        p = jnp.exp(s - m)
        l = jnp.sum(p, axis=-1, keepdims=True)
        o = _dot(p.astype(BF16), mv_ref[g].astype(BF16)) / l
        o_ref[g:g + 1, :] = jnp.sum(jnp.where(hm, o, 0.0), axis=0, keepdims=True)


def _mem_sample(z_main, q_col, mk, mv, heads):
    nb, n_mem, w = mk.shape
    assert nb % MEM_GROUP == 0 and heads <= SUBLANES
    kern = functools.partial(_mem_sample_kernel, heads=heads, hd=w // heads)
    return pl.pallas_call(
        kern,
        grid=(nb // MEM_GROUP,),
        in_specs=[pl.BlockSpec((MEM_GROUP, w), lambda i: (i, q_col)),
                  pl.BlockSpec((MEM_GROUP, n_mem, w), lambda i: (i, 0, 0)),
                  pl.BlockSpec((MEM_GROUP, n_mem, w), lambda i: (i, 0, 0))],
        out_specs=pl.BlockSpec((MEM_GROUP, w), lambda i: (i, 0)),
        out_shape=jax.ShapeDtypeStruct((nb, w), F32),
        compiler_params=_params("parallel"),
        name="mem_sample",
    )(z_main, mk, mv)


def _topk_select(count, shape, top):
    topf = float(top)

    def bit_step(s, thr):
        cand = jnp.where(s == 0, 0, thr | lax.shift_left(jnp.int32(1), 31 - s))
        cnt = count(lambda k, idx: k >= cand)
        return jnp.where(cnt >= topf, cand, thr)

    thr = lax.fori_loop(0, 32, bit_step, jnp.full(shape, INT_MIN, I32))
    thr = jnp.maximum(thr, KEY_NEG_INF + 1)
    n_gt = count(lambda k, idx: k > thr)
    n_ge = count(lambda k, idx: k >= thr)
    return thr, n_ge, topf - n_gt


def _tie_bound(count, shape, thr, need, n_ge, top, idx_bits):
    full = jnp.full(shape, (1 << idx_bits) - 1, I32)

    def search():
        def bit_step(s, pos):
            cand = pos + lax.shift_left(jnp.int32(1), idx_bits - 1 - s)
            cnt = count(lambda k, idx: (k == thr) & (idx < cand))
            return jnp.where(cnt < need, cand, pos)
        return lax.fori_loop(0, idx_bits, bit_step, jnp.zeros(shape, I32))

    excess = jnp.max(n_ge) > float(top)
    return lax.cond(excess, search, lambda: full)


DSA_BLOCK = 256
SEARCH_BLOCKS = 2


I16 = jnp.int16
I16_MIN = -(2 ** 15)
HI_NEG_INF = KEY_NEG_INF >> 16
LO_NEG_INF = ((KEY_NEG_INF & 0xFFFF) ^ 0x8000) - 0x10000 * (((KEY_NEG_INF & 0xFFFF) ^ 0x8000) >> 15)
PACKED_ROWS = 2 * SUBLANES


def _search16(count_ge, need, shape):
    def step(s, t):
        cand = jnp.where(s == 0, 0, t | lax.shift_left(jnp.int32(1), 15 - s))
        return jnp.where(count_ge(cand.astype(I16)) >= need, cand, t)
    return lax.fori_loop(0, 16, step, jnp.full(shape, I16_MIN, I32))


def _dsa_prompt_kernel(qt_ref, q6t_ref, wt_ref, k6_ref, k_ref, vt_ref, bias_ref, o_ref,
                       mask_ref, hi_ref, lo_ref, qm_ref, m_ref, acc_ref, s_ref, p_ref, a_ref,
                       *, heads, hd, idx_heads, top, seq):
    blk = DSA_BLOCK
    i = pl.program_id(1)
    pair = LANES // hd
    n_search = (i + SEARCH_BLOCKS) // SEARCH_BLOCKS
    krow = lax.broadcasted_iota(I32, (blk, blk), 0)
    qpos = i * blk + lax.broadcasted_iota(I32, (blk, blk), 1)
    shape = (1, blk)

    def score_block(j, _):
        kc = k6_ref[pl.ds(pl.multiple_of(j * blk, blk), blk), :]
        sc = jnp.zeros((blk, blk), F32)
        for h in range(idx_heads):
            sc = sc + wt_ref[0, h:h + 1, :] * jnp.maximum(_dot(kc, q6t_ref[0, h]), 0.0)
        key = _float_key(jnp.where(j * blk + krow <= qpos, sc, -jnp.inf))
        hi_ref[j] = lax.shift_right_arithmetic(key, 16).astype(I16)
        lo_ref[j] = (key ^ 0x8000).astype(I16)
        return 0

    lax.fori_loop(0, n_search * SEARCH_BLOCKS, score_block, 0)

    def idx16(j):
        return (j * blk + krow).astype(I16)

    def count(pred):
        def body(c, acc):
            for b in range(SEARCH_BLOCKS):
                j = c * SEARCH_BLOCKS + b
                ind = jnp.where(pred(hi_ref[j], lo_ref[j], j), jnp.ones((), I16), jnp.zeros((), I16))
                for r in range(blk // PACKED_ROWS):
                    acc = acc + ind[r * PACKED_ROWS:(r + 1) * PACKED_ROWS, :]
            return acc
        acc = lax.fori_loop(0, n_search, body, jnp.zeros((PACKED_ROWS, blk), I16))
        return jnp.sum(acc.astype(I32), axis=0, keepdims=True)

    hi_t = _search16(lambda c: count(lambda hi, lo, j: hi >= c), top, shape)
    hi_t16 = hi_t.astype(I16)
    n_above = count(lambda hi, lo, j: hi > hi_t16)
    n_band = count(lambda hi, lo, j: hi == hi_t16)

    def band_block(j, _):
        lo_ref[j] = jnp.where(hi_ref[j] == hi_t16, lo_ref[j], jnp.full((), I16_MIN, I16))
        return 0

    lax.fori_loop(0, n_search * SEARCH_BLOCKS, band_block, 0)
    lo_t = _search16(lambda c: count(lambda hi, lo, j: lo >= c), top - n_above, shape)
    lo_t16 = lo_t.astype(I16)
    n_gt = n_above + count(lambda hi, lo, j: lo > lo_t16)
    n_ge = n_above + jnp.where(lo_t == I16_MIN, n_band, count(lambda hi, lo, j: lo >= lo_t16))
    need = top - n_gt
    short = hi_t <= HI_NEG_INF
    hi_t16 = jnp.where(short, HI_NEG_INF, hi_t).astype(I16)
    lo_t16 = jnp.where(short, LO_NEG_INF, lo_t).astype(I16)
    n_ge = jnp.where(short, 0, n_ge)

    idx_bits = (seq - 1).bit_length()

    def tie_search():
        def step(s, pos):
            cand = pos + lax.shift_left(jnp.int32(1), idx_bits - 1 - s)
            c16 = cand.astype(I16)
            cnt = count(lambda hi, lo, j: (hi == hi_t16) & (lo == lo_t16) & (idx16(j) < c16))
            return jnp.where(cnt < need, cand, pos)
        return lax.fori_loop(0, idx_bits, step, jnp.zeros(shape, I32))

    pos = lax.cond(jnp.max(n_ge) > top, tie_search, lambda: jnp.full(shape, (1 << idx_bits) - 1, I32))
    pos16 = jnp.where(short, -1, pos).astype(I16)

    def mask_block(j, _):
        hi, lo = hi_ref[j], lo_ref[j]
        tie = (lo == lo_t16) & (idx16(j) <= pos16)
        sel = (hi > hi_t16) | ((hi == hi_t16) & ((lo > lo_t16) | tie))
        mask_ref[j] = jnp.where(sel, jnp.zeros((), BF16), jnp.full((), NEG_INF, BF16)).astype(F32)
        return 0

    lax.fori_loop(0, i + 1, mask_block, 0)

    sub = lax.broadcasted_iota(I32, (LANES, blk), 0)
    for h in range(heads):
        g = h // pair
        own = (sub >= (h % pair) * hd) & (sub < (h % pair + 1) * hd)
        qm_ref[h] = jnp.where(own, qt_ref[0, g * LANES:(g + 1) * LANES, :], jnp.zeros((), BF16))

    m_ref[...] = jnp.full(m_ref.shape, NEG_INF, F32)
    acc_ref[...] = jnp.zeros(acc_ref.shape, F32)

    def attend(j, near):
        rows = pl.ds(pl.multiple_of(j * blk, blk), blk)
        addm = mask_ref[j]
        for h in range(heads):
            g = h // pair
            s = _dot(k_ref[rows, g * LANES:(g + 1) * LANES], qm_ref[h]) + addm
            if near is not None:
                s = s + bias_ref[near, h]
            s_ref[h] = s
            m_old = m_ref[h]
            m_new = jnp.maximum(m_old, jnp.max(s, axis=0, keepdims=True))
            a_ref[h] = jnp.exp2(m_old - m_new)
            m_ref[h] = m_new
        for h in range(heads):
            p_ref[h] = jnp.exp2(s_ref[h] - m_ref[h]).astype(BF16)
        for h in range(heads):
            acc_ref[h] = a_ref[h] * acc_ref[h] + _dot(vt_ref[0, j, h], p_ref[h])

    def far(j, _):
        attend(j, None)
        return 0

    lax.fori_loop(0, i - 1, far, 0)

    @pl.when(i >= 1)
    def _():
        attend(i - 1, 1)

    attend(i, 0)

    for g in range(heads // pair):
        o = jnp.concatenate([acc_ref[g * pair + r, 0:hd, :] / acc_ref[g * pair + r, hd:hd + 1, :]
                             for r in range(pair)], axis=0)
        o_ref[:, g * LANES:(g + 1) * LANES] = o.T


def _dsa_prompt(qt, q6t, wt, k6, k_bf, vt, bias, nb, seq, heads, top):
    blk = DSA_BLOCK
    w = k_bf.shape[1]
    hd = w // heads
    idx_heads, kd = q6t.shape[1], q6t.shape[2]
    vrows = vt.shape[3]
    assert seq % (blk * SEARCH_BLOCKS) == 0 and LANES % hd == 0 and blk >= REL_MAX_DIST
    assert seq <= 2 ** 15 and vrows > hd
    nq = seq // blk
    once = pl.Buffered(1)
    kern = functools.partial(_dsa_prompt_kernel, heads=heads, hd=hd, idx_heads=idx_heads, top=top, seq=seq)
    return pl.pallas_call(
        kern,
        grid=(nb, nq),
        in_specs=[
            pl.BlockSpec((1, w, blk), lambda n, i: (n, 0, i)),
            pl.BlockSpec((1, idx_heads, kd, blk), lambda n, i: (n, 0, 0, i)),
            pl.BlockSpec((1, idx_heads, blk), lambda n, i: (n, 0, i)),
            pl.BlockSpec((seq, kd), lambda n, i: (n, 0), pipeline_mode=once),
            pl.BlockSpec((seq, w), lambda n, i: (n, 0), pipeline_mode=once),
            pl.BlockSpec((1, nq, heads, vrows, blk), lambda n, i: (n, 0, 0, 0, 0), pipeline_mode=once),
            pl.BlockSpec((2, heads, blk, blk), lambda n, i: (0, 0, 0, 0), pipeline_mode=once),
        ],
        out_specs=pl.BlockSpec((blk, w), lambda n, i: (n * nq + i, 0)),
        out_shape=jax.ShapeDtypeStruct((nb * seq, w), F32),
        scratch_shapes=[pltpu.VMEM((nq, blk, blk), F32),
                        pltpu.VMEM((nq, blk, blk), I16),
                        pltpu.VMEM((nq, blk, blk), I16),
                        pltpu.VMEM((heads, LANES, blk), BF16),
                        pltpu.VMEM((heads, 1, blk), F32),
                        pltpu.VMEM((heads, vrows, blk), F32),
                        pltpu.VMEM((heads, blk, blk), F32),
                        pltpu.VMEM((heads, blk, blk), BF16),
                        pltpu.VMEM((heads, 1, blk), F32)],
        compiler_params=_params("parallel", "arbitrary"),
        name="dsa_prompt",
    )(qt, q6t, wt, k6, k_bf, vt, bias)


def _dsa_sample_scores_kernel(pt_ref, q_ref, w_ref, kown_ref, *rest, n_pages, page):
    del pt_ref
    pages, o_ref = rest[:n_pages], rest[n_pages]
    q = q_ref[0]
    qh, qm, ql = _split3(q)
    w = w_ref[0]
    for p in range(n_pages):
        kh, km, kl = _split3(pages[p][0])
        d = (_dot(qh, kh) + _dot(qh, km) + _dot(qm, kh)
             + _dot(qh, kl) + _dot(qm, km) + _dot(ql, kh))
        sc = jnp.sum(w * jnp.maximum(d, 0.0), axis=0, keepdims=True)
        o_ref[0, :, p * page:(p + 1) * page] = sc
    d_own = jnp.sum(q * kown_ref[0], axis=-1, keepdims=True)
    s_own = jnp.sum(w * jnp.maximum(d_own, 0.0), axis=0, keepdims=True)
    lane = lax.broadcasted_iota(I32, (1, LANES), 1)
    o_ref[0, :, n_pages * page:] = jnp.where(lane == 0, s_own, -jnp.inf)


def _dsa_sample_scores(page_table, q_idx, w_idx, k_own, kidx_t):
    nb, n_pages = page_table.shape
    idx_dim, page = kidx_t.shape[1:]
    assert page % LANES == 0
    width = n_pages * page + LANES
    kern = functools.partial(_dsa_sample_scores_kernel, n_pages=n_pages, page=page)
    page_specs = [pl.BlockSpec((1, idx_dim, page), lambda n, pt, p=p: (pt[n * n_pages + p], 0, 0))
                  for p in range(n_pages)]
    grid_spec = pltpu.PrefetchScalarGridSpec(
        num_scalar_prefetch=1,
        grid=(nb,),
        in_specs=[pl.BlockSpec((1,) + q_idx.shape[1:], lambda n, pt: (n, 0, 0)),
                  pl.BlockSpec((1,) + w_idx.shape[1:], lambda n, pt: (n, 0, 0)),
                  pl.BlockSpec((1, 1, idx_dim), lambda n, pt: (n, 0, 0))] + page_specs,
        out_specs=pl.BlockSpec((1, 1, width), lambda n, pt: (n, 0, 0)),
    )
    return pl.pallas_call(
        kern,
        grid_spec=grid_spec,
        out_shape=jax.ShapeDtypeStruct((nb, 1, width), F32),
        compiler_params=_params("arbitrary"),
        name="dsa_sample_scores",
    )(page_table.reshape(-1), q_idx, w_idx, k_own, *([kidx_t] * n_pages))


def _dsa_sample_mask_kernel(s_ref, o_ref, *, top, width):
    keys = _float_key(s_ref[...])
    idx = lax.broadcasted_iota(I32, keys.shape, 1)

    def count(pred):
        return jnp.sum(jnp.where(pred(keys, idx), 1.0, 0.0), axis=-1, keepdims=True)

    shape = (keys.shape[0], 1)
    thr, n_ge, need = _topk_select(count, shape, top)
    pos = _tie_bound(count, shape, thr, need, n_ge, top, (width - 1).bit_length())
    sel = (keys > thr) | ((keys == thr) & (idx <= pos))
    o_ref[...] = jnp.where(sel, 0.0, NEG_INF)


def _dsa_sample_mask(scores, top):
    rows, width = scores.shape
    return pl.pallas_call(
        functools.partial(_dsa_sample_mask_kernel, top=top, width=width),
        out_shape=jax.ShapeDtypeStruct((rows, width), F32),
        compiler_params=pltpu.CompilerParams(vmem_limit_bytes=VMEM_LIMIT),
        name="dsa_sample_mask",
    )(scores)


def _dsa_sample_attend_kernel(pt_ref, q_ref, kown_ref, vown_ref, mask_ref, bias_ref, *rest,
                              n_pages, page, heads, hd):
    del pt_ref
    kp, vp, o_ref = rest[:n_pages], rest[n_pages:2 * n_pages], rest[2 * n_pages]
    past = n_pages * page
    hm = _head_mask(heads, hd)
    q_bd = jnp.where(hm, jnp.broadcast_to(q_ref[0] * (hd ** -0.5), hm.shape), 0.0)
    q_bf = q_bd.astype(BF16)
    logits = []
    for p in range(n_pages):
        sl = slice(p * page, (p + 1) * page)
        logits.append(_dot(q_bf, kp[p][0].astype(BF16)) + bias_ref[:, sl] + mask_ref[0, :, sl])
    s_own = (jnp.sum(q_bd * kown_ref[0], axis=-1, keepdims=True)
             + bias_ref[:, past:past + 1] + mask_ref[0, :, past:past + 1])
    m = s_own
    for s in logits:
        m = jnp.maximum(m, jnp.max(s, axis=-1, keepdims=True))
    p_own = jnp.exp(s_own - m)
    l = p_own
    acc = p_own * vown_ref[0]
    for p in range(n_pages):
        pr = jnp.exp(logits[p] - m)
        l = l + jnp.sum(pr, axis=-1, keepdims=True)
        acc = acc + _dot_nt(pr.astype(BF16), vp[p][0].astype(BF16))
    o_ref[0] = jnp.sum(jnp.where(hm, acc / l, 0.0), axis=0, keepdims=True)


def _dsa_sample_attend(page_table, q, k_own, v_own, mask, bias, k_t, v_t, heads):
    nb, n_pages = page_table.shape
    w, page = k_t.shape[1:]
    assert heads == SUBLANES
    width = mask.shape[-1]
    kern = functools.partial(_dsa_sample_attend_kernel, n_pages=n_pages, page=page, heads=heads, hd=w // heads)
    page_specs = [pl.BlockSpec((1, w, page), lambda n, pt, p=p: (pt[n * n_pages + p], 0, 0))
                  for p in range(n_pages)]
    row = pl.BlockSpec((1, 1, w), lambda n, pt: (n, 0, 0))
    grid_spec = pltpu.PrefetchScalarGridSpec(
        num_scalar_prefetch=1,
        grid=(nb,),
        in_specs=[row, row, row,
                  pl.BlockSpec((1, 1, width), lambda n, pt: (n, 0, 0)),
                  pl.BlockSpec((heads, width), lambda n, pt: (0, 0))] + page_specs + page_specs,
        out_specs=pl.BlockSpec((1, 1, w), lambda n, pt: (n, 0, 0)),
    )
    return pl.pallas_call(
        kern,
        grid_spec=grid_spec,
        out_shape=jax.ShapeDtypeStruct((nb, 1, w), F32),
        compiler_params=_params("arbitrary"),
        name="dsa_sample_attend",
    )(page_table.reshape(-1), q, k_own, v_own, mask, bias, *([k_t] * n_pages), *([v_t] * n_pages))


def _merge_kernel(x_ref, y_ref, a_ref, m_ref, g1_ref, g2_ref, g3_ref, wc_ref, wa_ref, wm_ref, wo_ref,
                  lg_ref, lb_ref, o_ref):
    c = _dot(y_ref[...].astype(BF16), wc_ref[...])
    a = _dot(a_ref[...].astype(BF16), wa_ref[...])
    m = _dot(m_ref[...].astype(BF16), wm_ref[...])
    merged = (jax.nn.sigmoid(g1_ref[...]) * c + jax.nn.sigmoid(g2_ref[...]) * a
              + jax.nn.sigmoid(g3_ref[...]) * m)
    h = _dot(merged.astype(BF16), wo_ref[...])
    o_ref[...] = _layer_norm(ALPHA * x_ref[...] + h, lg_ref[...], lb_ref[...])


def _merge(x, y_act, att, mem, z_main, g_col, wc, wa, wm, wo, ln_g, ln_b):
    n, d = x.shape
    tm = _pick_tile(n, (256, 128, 64, 32, 16, 8))
    row = lambda width: pl.BlockSpec((tm, width), lambda i: (i, 0))
    gate = lambda b: pl.BlockSpec((tm, d), lambda i: (i, g_col + b))
    full = lambda a: pl.BlockSpec(a.shape, lambda i: (0, 0))
    return pl.pallas_call(
        _merge_kernel,
        grid=(n // tm,),
        in_specs=[row(d), row(y_act.shape[1]), row(att.shape[1]), row(mem.shape[1]),
                  gate(0), gate(1), gate(2), full(wc), full(wa), full(wm), full(wo), full(ln_g), full(ln_b)],
        out_specs=row(d),
        out_shape=jax.ShapeDtypeStruct((n, d), F32),
        compiler_params=_params("parallel"),
        name="merge_ln1",
    )(x, y_act, att, mem, z_main, z_main, z_main, wc, wa, wm, wo, ln_g, ln_b)


MOE_BLOCK = 256
MOE_TOKENS = 128


U32 = jnp.uint32
HIGH_HALF = 0xFFFF0000


def _pack_halves(x):
    c = x.shape[1] // 2
    lo = lax.bitcast_convert_type(x[:, :c].astype(BF16).astype(F32), U32) >> 16
    hi = lax.bitcast_convert_type(x[:, c:].astype(BF16).astype(F32), U32) & jnp.uint32(HIGH_HALF)
    return lo | hi


def _unpack_halves(u):
    return (lax.bitcast_convert_type(u << 16, F32),
            lax.bitcast_convert_type(u & jnp.uint32(HIGH_HALF), F32))


def _router_kernel(x_ref, wh_ref, wl_ref, rb_ref, e_ref, g_ref, r_ref, cnt_ref, xp_ref, carry_ref, *, topk):
    i = pl.program_id(0)

    @pl.when(i == 0)
    def _():
        carry_ref[...] = jnp.zeros(carry_ref.shape, F32)

    xp_ref[...] = _pack_halves(x_ref[...])
    xh, xl = _split2(x_ref[...])
    logits = _dot(xh, wh_ref[...]) + _dot(xl, wh_ref[...]) + _dot(xh, wl_ref[...])
    scores = jax.nn.sigmoid(logits)
    tm, ne = scores.shape
    sel = scores + rb_ref[...]
    lane = lax.broadcasted_iota(I32, (tm, ne), 1).astype(F32)
    chosen = jnp.zeros((tm, ne), F32)
    picks = []
    total = jnp.zeros((tm, 1), F32)
    for _ in range(topk):
        mx = jnp.max(sel, axis=-1, keepdims=True)
        idx = jnp.min(jnp.where(sel == mx, lane, float(ne)), axis=-1, keepdims=True)
        hot = lane == idx
        sk = jnp.sum(jnp.where(hot, scores, 0.0), axis=-1, keepdims=True)
        sel = jnp.where(hot, -jnp.inf, sel)
        chosen = jnp.where(hot, 1.0, chosen)
        total = total + sk
        picks.append((idx, sk, hot))
    r = lax.broadcasted_iota(I32, (tm, tm), 0)
    c = lax.broadcasted_iota(I32, (tm, tm), 1)
    before = jnp.where(c < r, 1.0, 0.0).astype(BF16)
    rank = _dot(before, chosen.astype(BF16)) + carry_ref[...]
    out_lane = lax.broadcasted_iota(I32, (tm, LANES), 1)
    e_out = jnp.zeros((tm, LANES), I32)
    g_out = jnp.zeros((tm, LANES), F32)
    r_out = jnp.zeros((tm, LANES), I32)
    for k, (idx, sk, hot) in enumerate(picks):
        rk = jnp.sum(jnp.where(hot, rank, 0.0), axis=-1, keepdims=True)
        e_out = jnp.where(out_lane == k, idx.astype(I32), e_out)
        g_out = jnp.where(out_lane == k, sk / total * ROUTED_SCALE, g_out)
        r_out = jnp.where(out_lane == k, rk.astype(I32), r_out)
    e_ref[...] = e_out
    g_ref[...] = g_out
    r_ref[...] = r_out
    carry_ref[...] = carry_ref[...] + jnp.sum(chosen, axis=0, keepdims=True)
    cnt_ref[...] = carry_ref[...]


def _router(x1, w_router, router_bias):
    n, d = x1.shape
    ne = w_router.shape[1]
    tm = _pick_tile(n, (256, 128, 64, 48, 32, 16, 8))
    wh, wl = _split2(w_router)
    out = lambda: pl.BlockSpec((tm, LANES), lambda i: (i, 0))
    return pl.pallas_call(
        functools.partial(_router_kernel, topk=TOP_K),
        grid=(n // tm,),
        in_specs=[pl.BlockSpec((tm, d), lambda i: (i, 0)),
                  pl.BlockSpec((d, ne), lambda i: (0, 0)),
                  pl.BlockSpec((d, ne), lambda i: (0, 0)),
                  pl.BlockSpec((1, ne), lambda i: (0, 0))],
        out_specs=[out(), out(), out(), pl.BlockSpec((1, ne), lambda i: (0, 0)),
                   pl.BlockSpec((tm, d // 2), lambda i: (i, 0))],
        out_shape=[jax.ShapeDtypeStruct((n, LANES), I32), jax.ShapeDtypeStruct((n, LANES), F32),
                   jax.ShapeDtypeStruct((n, LANES), I32), jax.ShapeDtypeStruct((1, ne), F32),
                   jax.ShapeDtypeStruct((n, d // 2), U32)],
        scratch_shapes=[pltpu.VMEM((1, ne), F32)],
        compiler_params=_params("arbitrary"),
        name="router",
    )(x1, wh, wl, router_bias)


def _slots_kernel(e_ref, r_ref, ps_ref, o_ref, *, topk):
    e, r = e_ref[...], r_ref[...]
    tm, ne = e.shape[0], ps_ref.shape[1]
    lane = lax.broadcasted_iota(I32, (tm, ne), 1)
    out_lane = lax.broadcasted_iota(I32, e.shape, 1)
    out = jnp.zeros(e.shape, I32)
    for k in range(topk):
        start = jnp.sum(jnp.where(lane == e[:, k:k + 1], ps_ref[...], 0.0), axis=-1, keepdims=True)
        out = jnp.where(out_lane == k, start.astype(I32) + r, out)
    o_ref[...] = out


def _slots(top_e, rank, pstart):
    n = top_e.shape[0]
    tm = _pick_tile(n, (256, 128, 64, 48, 32, 16, 8))
    row = pl.BlockSpec((tm, LANES), lambda i: (i, 0))
    return pl.pallas_call(
        functools.partial(_slots_kernel, topk=TOP_K),
        grid=(n // tm,),
        in_specs=[row, row, pl.BlockSpec(pstart.shape, lambda i: (0, 0))],
        out_specs=row,
        out_shape=jax.ShapeDtypeStruct((n, LANES), I32),
        compiler_params=_params("parallel"),
        name="moe_slots",
    )(top_e, rank, pstart)


def _dispatch_kernel(dest_ref, x_ref, init_ref, xs_ref, sem, *, tt, topk):
    del init_ref

    def copy(t, d):
        return pltpu.make_async_copy(x_ref.at[pl.ds(t, 1)], xs_ref.at[pl.ds(d, 1)], sem)

    def start(t, _):
        for k in range(topk):
            copy(t, dest_ref[t * topk + k]).start()
        return 0

    lax.fori_loop(0, tt, start, 0)

    def wait(t, _):
        for k in range(topk):
            copy(t, dest_ref[t * topk + k]).wait()
        return 0

    lax.fori_loop(0, tt, wait, 0)


def _dispatch(x1, dest, cap):
    n, d = x1.shape
    topk = dest.shape[1]
    tt = _pick_tile(n, (MOE_TOKENS, 64, 48, 32, 16, 8))
    init = jnp.zeros((cap, d), x1.dtype)
    return pl.pallas_call(
        functools.partial(_dispatch_kernel, tt=tt, topk=topk),
        grid=(n // tt,),
        in_specs=[pl.BlockSpec((tt * topk,), lambda i: (i,), memory_space=pltpu.SMEM),
                  pl.BlockSpec((tt, d), lambda i: (i, 0)),
                  pl.BlockSpec(memory_space=pl.ANY)],
        out_specs=pl.BlockSpec(memory_space=pl.ANY),
        out_shape=jax.ShapeDtypeStruct((cap, d), x1.dtype),
        scratch_shapes=[pltpu.SemaphoreType.DMA(())],
        input_output_aliases={2: 0},
        compiler_params=_params("arbitrary"),
        name="moe_dispatch",
    )(dest.reshape(-1), x1, init)


def _expert_kernel(be_ref, na_ref, xs_ref, wgu_ref, wdn_ref, y_ref, wgu_bf, wdn_bf, *, de):
    b = pl.program_id(0)

    @pl.when(b < na_ref[0])
    def _():
        @pl.when((b == 0) | (be_ref[b] != be_ref[jnp.maximum(b - 1, 0)]))
        def _():
            wgu_bf[...] = wgu_ref[0].astype(BF16)
            wdn_bf[...] = wdn_ref[0].astype(BF16)

        lo, hi = _unpack_halves(xs_ref[...])
        half = lo.shape[1]
        h = _dot(lo.astype(BF16), wgu_bf[0:half, :]) + _dot(hi.astype(BF16), wgu_bf[half:, :])
        act = _silu(h[:, :de]) * h[:, de:]
        y_ref[...] = _pack_halves(_dot(act.astype(BF16), wdn_bf[...]))

    @pl.when(b >= na_ref[0])
    def _():
        y_ref[...] = jnp.zeros(y_ref.shape, U32)


def _experts(xs, blk_exp, n_active, w_gu, w_down):
    cap, dp = xs.shape
    ne, d, de2 = w_gu.shape
    de = de2 // 2
    n_blk = cap // MOE_BLOCK
    grid_spec = pltpu.PrefetchScalarGridSpec(
        num_scalar_prefetch=2,
        grid=(n_blk,),
        in_specs=[pl.BlockSpec((MOE_BLOCK, dp), lambda b, be, na: (b, 0)),
                  pl.BlockSpec((1, d, de2), lambda b, be, na: (be[b], 0, 0)),
                  pl.BlockSpec((1, de, d), lambda b, be, na: (be[b], 0, 0))],
        out_specs=pl.BlockSpec((MOE_BLOCK, dp), lambda b, be, na: (b, 0)),
        scratch_shapes=[pltpu.VMEM((d, de2), BF16), pltpu.VMEM((de, d), BF16)],
    )
    return pl.pallas_call(
        functools.partial(_expert_kernel, de=de),
        grid_spec=grid_spec,
        out_shape=jax.ShapeDtypeStruct((cap, dp), U32),
        compiler_params=_params("arbitrary"),
        name="moe_experts",
    )(blk_exp, n_active, xs, w_gu, w_down)


def _combine_kernel(dest_ref, gate_ref, x_ref, y_ref, wgu_ref, wdn_ref, lg_ref, lb_ref, o_ref, buf, sem,
                    *, tt, topk, ds):
    def copy(t, k):
        return pltpu.make_async_copy(y_ref.at[pl.ds(dest_ref[t * topk + k], 1)], buf.at[k, pl.ds(t, 1)], sem)

    def start(t, _):
        for k in range(topk):
            copy(t, k).start()
        return 0

    lax.fori_loop(0, tt, start, 0)

    x = x_ref[...]
    h = _dot(x.astype(BF16), wgu_ref[...])
    shared = _dot((_silu(h[:, :ds]) * h[:, ds:]).astype(BF16), wdn_ref[...])

    def wait(t, _):
        for k in range(topk):
            copy(t, k).wait()
        return 0

    lax.fori_loop(0, tt, wait, 0)

    r_lo = jnp.zeros(buf.shape[1:], F32)
    r_hi = jnp.zeros(buf.shape[1:], F32)
    for k in range(topk):
        lo, hi = _unpack_halves(buf[k])
        r_lo = r_lo + gate_ref[:, k:k + 1] * lo
        r_hi = r_hi + gate_ref[:, k:k + 1] * hi
    routed = jnp.concatenate([r_lo, r_hi], axis=-1)
    o_ref[...] = _layer_norm(ALPHA * x + routed + shared, lg_ref[...], lb_ref[...])


def _combine(x1, y_rows, dest, gates, w_sh_gu, w_sh_down, ln_g, ln_b):
    n, d = x1.shape
    topk = dest.shape[1]
    tt = _pick_tile(n, (MOE_TOKENS, 64, 48, 32, 16, 8))
    full = lambda a: pl.BlockSpec(a.shape, lambda i: (0, 0))
    return pl.pallas_call(
        functools.partial(_combine_kernel, tt=tt, topk=topk, ds=w_sh_down.shape[0]),
        grid=(n // tt,),
        in_specs=[pl.BlockSpec((tt * topk,), lambda i: (i,), memory_space=pltpu.SMEM),
                  pl.BlockSpec((tt, LANES), lambda i: (i, 0)),
                  pl.BlockSpec((tt, d), lambda i: (i, 0)),
                  pl.BlockSpec(memory_space=pl.ANY),
                  full(w_sh_gu), full(w_sh_down), full(ln_g), full(ln_b)],
        out_specs=pl.BlockSpec((tt, d), lambda i: (i, 0)),
        out_shape=jax.ShapeDtypeStruct((n, d), F32),
        scratch_shapes=[pltpu.VMEM((topk, tt, y_rows.shape[1]), U32), pltpu.SemaphoreType.DMA(())],
        compiler_params=_params("arbitrary"),
        name="moe_combine",
    )(dest.reshape(-1), gates, x1, y_rows, w_sh_gu, w_sh_down, ln_g, ln_b)


def _moe_ln2(x1, w_router, router_bias, w_exp_gu, w_exp_down, w_sh_gu, w_sh_down, ln_g, ln_b):
    n, d = x1.shape
    ne = w_router.shape[1]
    top_e, gates, rank, counts, x1_packed = _router(x1, w_router, router_bias)
    counts = counts[0].astype(I32)
    padded = (counts + MOE_BLOCK - 1) // MOE_BLOCK * MOE_BLOCK
    pend = jnp.cumsum(padded)
    pstart = pend - padded
    dest = _slots(top_e, rank, pstart.astype(F32)[None, :])[:, :TOP_K]
    cap = (n * TOP_K + ne * (MOE_BLOCK - 1) + MOE_BLOCK - 1) // MOE_BLOCK * MOE_BLOCK
    n_blk = cap // MOE_BLOCK
    blk_exp = jnp.minimum(jnp.searchsorted(pend, jnp.arange(n_blk, dtype=I32) * MOE_BLOCK, side='right'),
                          ne - 1).astype(I32)
    n_active = (pend[-1:] // MOE_BLOCK).astype(I32)
    xs = _dispatch(x1_packed, dest, cap)
    y_rows = _experts(xs, blk_exp, n_active, w_exp_gu, w_exp_down)
    return _combine(x1, y_rows, dest, gates, w_sh_gu.astype(BF16), w_sh_down.astype(BF16), ln_g, ln_b)


def _rel_bucket(dist):
    n = jnp.maximum(dist, 0)
    max_exact = N_BUCKETS // 2
    nf = jnp.maximum(n, max_exact).astype(F32)
    large = max_exact + (jnp.log(nf / max_exact) / math.log(REL_MAX_DIST / max_exact)
                         * (N_BUCKETS - max_exact)).astype(I32)
    return jnp.where(n < max_exact, n, jnp.minimum(large, N_BUCKETS - 1))


def _pages_t(cache):
    n_pool, page = cache.shape[:2]
    return jnp.moveaxis(cache, 1, -1).reshape(n_pool, -1, page)


def _bias_table(rel_bias, dist):
    hit = _rel_bucket(dist)[None, :, None] == jnp.arange(N_BUCKETS, dtype=I32)[None, None, :]
    return jnp.sum(jnp.where(hit, rel_bias.T[:, None, :], 0.0), axis=-1)


def kernel(x_prompt, x_sample, mem_prompt, cache_k, cache_v, cache_kidx, cache_mem_k, cache_mem_v, state_conv, page_table, rel_bias, w_in, b_in, conv_w, conv_b, conv_ln_g, conv_ln_b, w_conv_out, w_attn_out, w_mem_kv, w_mem_out, w_out, ln1_g, ln1_b, w_router, router_bias, w_exp_gu, w_exp_down, w_sh_gu, w_sh_down, ln2_g, ln2_b):
    assert w_in.shape[0] == DEPTH == 1
    nb, seq, d = x_prompt.shape
    nsb, nst, _ = x_sample.shape
    assert nst == 1
    width, dc = conv_w.shape[1:]
    _, n_pool, page, heads, hd = cache_k.shape
    idx_dim = cache_kidx.shape[-1]
    n_mem, mem_heads, mem_hd = cache_mem_k.shape[2:]
    aw, mw, iw = heads * hd, mem_heads * mem_hd, N_IDX_HEADS * idx_dim
    assert 2 * dc == d and aw * 2 == d and mw * 2 == d

    o_q = 2 * dc
    o_qi = o_q + 3 * aw
    o_ki = o_qi + iw
    o_wi = o_ki + idx_dim
    o_qm = o_wi + N_IDX_HEADS
    o_g = o_qm + mw
    w0, b0 = w_in[0], b_in[0]
    w_main = jnp.concatenate([w0[:, :o_qi], w0[:, o_qm:]], axis=1).astype(BF16)
    b_main = jnp.concatenate([b0[:o_qi], b0[o_qm:]])[None, :]
    n_idx = o_qm - o_qi
    n_idx_pad = -(-n_idx // LANES) * LANES
    w_idx = jnp.pad(w0[:, o_qi:o_qm], ((0, 0), (0, n_idx_pad - n_idx)))
    b_idx = jnp.pad(b0[o_qi:o_qm], (0, n_idx_pad - n_idx))[None, :]
    col_q, col_k, col_v, col_qm, col_g = 2, 3, 4, 5, 3

    xp = x_prompt.reshape(nb * seq, d)
    xs = x_sample.reshape(nsb, d)
    zp = _linear(xp, w_main, b_main, tn=1024)
    zs = _linear(xs, w_main, b_main, tn=1024)
    zip_ = _linear(xp, w_idx, b_idx, split=True, tn=n_idx_pad)
    zis = _linear(xs, w_idx, b_idx, split=True, tn=n_idx_pad)

    kp, vp = zp[:, col_k * aw:(col_k + 1) * aw], zp[:, col_v * aw:(col_v + 1) * aw]
    ks, vs = zs[:, col_k * aw:(col_k + 1) * aw], zs[:, col_v * aw:(col_v + 1) * aw]
    kip, kis = zip_[:, iw:iw + idx_dim], zis[:, iw:iw + idx_dim]

    conv0 = jnp.zeros((nb, width - 1, dc), F32)
    cw, cb, cg, cbb = conv_w[0], conv_b[0][None, :], conv_ln_g[0][None, :], conv_ln_b[0][None, :]
    yp, tail_p = _conv_prompt(zp, conv0, cw, cb, cg, cbb, nb, seq)
    conv_p = tail_p[:, CONV_HALO - (width - 1):, :]
    ys, new_t = _conv_sample(zs, jnp.swapaxes(state_conv[0], 0, 1), cw, cb, cg, cbb)
    conv_s = jnp.swapaxes(new_t, 0, 1)

    wkv = w_mem_kv[0].astype(BF16)
    mkv = _linear(mem_prompt.reshape(nb * n_mem, d), wkv, jnp.zeros((1, 2 * mw), F32))
    mk_p, mv_p = mkv[:, :mw], mkv[:, mw:]
    mem_p = _mem_prompt(zp, col_qm, mk_p, mv_p, nb, seq, mem_heads)
    mem_s = _mem_sample(zs, col_qm, cache_mem_k[0].reshape(nsb, n_mem, mw),
                        cache_mem_v[0].reshape(nsb, n_mem, mw), mem_heads)

    far = _bias_table(rel_bias, jnp.full((1,), 4 * REL_MAX_DIST, I32))
    a = jnp.arange(DSA_BLOCK, dtype=I32)
    dist = jnp.stack([a[None, :] - a[:, None], DSA_BLOCK + a[None, :] - a[:, None]])
    bias_p = (_bias_table(rel_bias, dist.reshape(-1)) - far).reshape(heads, 2, DSA_BLOCK, DSA_BLOCK)
    bias_p = jnp.swapaxes(bias_p, 0, 1) * LOG2E
    qh, qm_, ql = _split3(zip_[:, :iw].reshape(nb, seq, N_IDX_HEADS, idx_dim))
    q6t = jnp.transpose(jnp.concatenate([qh, qh, qm_, qh, qm_, ql], axis=-1), (0, 2, 3, 1))
    kh, km, kl = _split3(kip)
    k6 = jnp.concatenate([kh, km, kh, kl, km, kh], axis=-1)
    wt = jnp.swapaxes(zip_[:, iw + idx_dim:iw + idx_dim + N_IDX_HEADS].reshape(nb, seq, N_IDX_HEADS), 1, 2)
    qt = jnp.swapaxes((zp[:, col_q * aw:(col_q + 1) * aw] * (hd ** -0.5 * LOG2E)).astype(BF16)
                      .reshape(nb, seq, aw), 1, 2)
    nkb = seq // DSA_BLOCK
    vt = jnp.transpose(vp.astype(BF16).reshape(nb, nkb, DSA_BLOCK, heads, hd), (0, 1, 3, 4, 2))
    vt = jnp.concatenate([vt, jnp.ones((nb, nkb, heads, 1, DSA_BLOCK), BF16),
                          jnp.zeros((nb, nkb, heads, PACKED_ROWS - 1, DSA_BLOCK), BF16)], axis=3)
    top_p = min(TOPK_MAX, seq // 4)
    att_p = _dsa_prompt(qt, q6t, wt, k6, kp.astype(BF16), vt, bias_p, nb, seq, heads, top_p)

    n_pages = page_table.shape[1]
    past = n_pages * page
    top_s = min(TOPK_MAX, (past + 1) // 4)
    scores = _dsa_sample_scores(page_table, zis[:, :iw].reshape(nsb, N_IDX_HEADS, idx_dim),
                                zis[:, iw + idx_dim:iw + idx_dim + N_IDX_HEADS].reshape(nsb, N_IDX_HEADS, 1),
                                kis.reshape(nsb, 1, idx_dim), _pages_t(cache_kidx[0]))
    mask_s = _dsa_sample_mask(scores.reshape(nsb, past + LANES), top_s)
    spos = jnp.arange(past + LANES, dtype=I32)
    bias_s = _bias_table(rel_bias, past - spos)
    att_s = _dsa_sample_attend(page_table, zs[:, col_q * aw:(col_q + 1) * aw].reshape(nsb, 1, aw),
                               ks.reshape(nsb, 1, aw), vs.reshape(nsb, 1, aw),
                               mask_s.reshape(nsb, 1, past + LANES), bias_s,
                               _pages_t(cache_k[0]), _pages_t(cache_v[0]), heads).reshape(nsb, aw)

    wc, wa, wm, wo = (w_conv_out[0].astype(BF16), w_attn_out[0].astype(BF16),
                      w_mem_out[0].astype(BF16), w_out[0].astype(BF16))
    l1g, l1b = ln1_g[0][None, :], ln1_b[0][None, :]
    x1p = _merge(xp, yp, att_p, mem_p, zp, col_g, wc, wa, wm, wo, l1g, l1b)
    x1s = _merge(xs, ys, att_s, mem_s, zs, col_g, wc, wa, wm, wo, l1g, l1b)
    x1 = jnp.concatenate([x1p, x1s], axis=0)
    x2 = _moe_ln2(x1, w_router[0], router_bias[0][None, :], w_exp_gu[0], w_exp_down[0],
                  w_sh_gu[0], w_sh_down[0], ln2_g[0][None, :], ln2_b[0][None, :])

    return (x2[:nb * seq].reshape(nb, seq, d), x2[nb * seq:].reshape(nsb, 1, d),
            kp.reshape(1, nb, seq, heads, hd), vp.reshape(1, nb, seq, heads, hd),
            kip.reshape(1, nb, seq, idx_dim),
            mk_p.reshape(1, nb, n_mem, mem_heads, mem_hd), mv_p.reshape(1, nb, n_mem, mem_heads, mem_hd),
            conv_p[None],
            ks.reshape(1, nsb, 1, heads, hd), vs.reshape(1, nsb, 1, heads, hd),
            kis.reshape(1, nsb, 1, idx_dim), conv_s[None])
```

```python
import functools
import math

import jax
import jax.numpy as jnp
from jax import lax
from jax.experimental import pallas as pl
from jax.experimental.pallas import tpu as pltpu

F32 = jnp.float32
BF16 = jnp.bfloat16
I32 = jnp.int32

DEPTH = 1
N_IDX_HEADS = 8
TOPK_MAX = 256
N_BUCKETS = 32
REL_MAX_DIST = 128
TOP_K = 8
ROUTED_SCALE = 2.5
ALPHA = (2 * DEPTH) ** 0.25
LN_EPS = 1e-5
NEG_INF = -1e30
LOG2E = math.log2(math.e)

LANES = 128
SUBLANES = 8
VMEM_LIMIT = 56 * 1024 * 1024

INT_MIN = -(2 ** 31)
KEY_NEG_INF = -2139095041

_dot = functools.partial(jnp.dot, preferred_element_type=F32)


def _dot_nt(a, b):
    return lax.dot_general(a, b, (((1,), (1,)), ((), ())), preferred_element_type=F32)


def _pick_tile(n, cands=(1024, 512, 384, 256, 128, 64, 48, 32, 16, 8)):
    for c in cands:
        if n % c == 0:
            return c
    raise ValueError(f"no row tile divides {n}")


def _params(*sem):
    return pltpu.CompilerParams(dimension_semantics=sem, vmem_limit_bytes=VMEM_LIMIT)


def _trunc_bf16(x):
    bits = lax.bitcast_convert_type(x, jnp.uint32) & jnp.uint32(0xFFFF0000)
    return lax.bitcast_convert_type(bits, F32)


def _split2(x):
    hi = _trunc_bf16(x)
    return hi.astype(BF16), (x - hi).astype(BF16)


def _split3(x):
    hi = _trunc_bf16(x)
    r = x - hi
    mid = _trunc_bf16(r)
    lo = r - mid
    return hi.astype(BF16), mid.astype(BF16), lo.astype(BF16)


def _layer_norm(x, g, b):
    mu = jnp.mean(x, axis=-1, keepdims=True)
    xc = x - mu
    var = jnp.mean(xc * xc, axis=-1, keepdims=True)
    return xc * lax.rsqrt(var + LN_EPS) * g + b


def _silu(x):
    return x * jax.nn.sigmoid(x)


def _float_key(x):
    x = jnp.where(x == 0.0, 0.0, x)
    bits = lax.bitcast_convert_type(x, I32)
    return jnp.where(bits >= 0, bits, bits ^ jnp.int32(0x7FFFFFFF))


def _linear_kernel(x_ref, w_ref, b_ref, o_ref, *, split):
    x = x_ref[...]
    if split:
        xh, xl = _split2(x)
        wh, wl = _split2(w_ref[...])
        acc = _dot(xh, wh) + _dot(xl, wh) + _dot(xh, wl)
    else:
        acc = _dot(x.astype(BF16), w_ref[...])
    o_ref[...] = acc + b_ref[...]


def _linear(x, w, b, *, split=False, tn=512):
    m, k = x.shape
    n = w.shape[1]
    tm = _pick_tile(m)
    tn = min(tn, n)
    assert n % tn == 0
    return pl.pallas_call(
        functools.partial(_linear_kernel, split=split),
        grid=(m // tm, n // tn),
        in_specs=[pl.BlockSpec((tm, k), lambda i, j: (i, 0)),
                  pl.BlockSpec((k, tn), lambda i, j: (0, j)),
                  pl.BlockSpec((1, tn), lambda i, j: (0, j))],
        out_specs=pl.BlockSpec((tm, tn), lambda i, j: (i, j)),
        out_shape=jax.ShapeDtypeStruct((m, n), F32),
        compiler_params=_params("parallel", "parallel"),
        name="linear_split" if split else "linear",
    )(x, w, b)


CONV_HALO = 32
CONV_SUB = 64


def _glu(z, dc):
    return z[:, :dc] * jax.nn.sigmoid(z[:, dc:])


def _conv_prompt_kernel(glu_ref, prev_ref, buf_ref, cw_ref, cb_ref, g_ref, b_ref, y_ref, tail_ref, ext_ref,
                        *, ts, width, dc):
    i = pl.program_id(1)
    u = _glu(glu_ref[...], dc)
    ext_ref[0:CONV_HALO, :] = _glu(prev_ref[...], dc)

    @pl.when(i == 0)
    def _():
        ext_ref[CONV_HALO - (width - 1):CONV_HALO, :] = buf_ref[0]

    ext_ref[CONV_HALO:CONV_HALO + ts, :] = u
    tail_ref[0] = u[ts - CONV_HALO:, :]
    off = CONV_HALO - (width - 1)
    for r in range(ts // CONV_SUB):
        acc = jnp.broadcast_to(cb_ref[...], (CONV_SUB, dc))
        for j in range(width):
            lo = r * CONV_SUB + off + j
            acc = acc + ext_ref[lo:lo + CONV_SUB, :] * cw_ref[j:j + 1, :]
        y_ref[r * CONV_SUB:(r + 1) * CONV_SUB, :] = _silu(_layer_norm(acc, g_ref[...], b_ref[...]))


def _conv_prompt(z_main, conv_buf, conv_w, conv_b, ln_g, ln_b, nb, seq):
    width, dc = conv_w.shape
    assert width - 1 <= CONV_HALO
    ts = 256
    assert seq % ts == 0 and ts % CONV_SUB == 0 and ts % CONV_HALO == 0
    nt = seq // ts
    kern = functools.partial(_conv_prompt_kernel, ts=ts, width=width, dc=dc)
    return pl.pallas_call(
        kern,
        grid=(nb, nt),
        in_specs=[
            pl.BlockSpec((ts, 2 * dc), lambda n, i: (n * nt + i, 0)),
            pl.BlockSpec((CONV_HALO, 2 * dc),
                         lambda n, i: (jnp.maximum((n * nt + i) * (ts // CONV_HALO) - 1, 0), 0)),
            pl.BlockSpec((1, width - 1, dc), lambda n, i: (n, 0, 0)),
            pl.BlockSpec((width, dc), lambda n, i: (0, 0)),
            pl.BlockSpec((1, dc), lambda n, i: (0, 0)),
            pl.BlockSpec((1, dc), lambda n, i: (0, 0)),
            pl.BlockSpec((1, dc), lambda n, i: (0, 0)),
        ],
        out_specs=[pl.BlockSpec((ts, dc), lambda n, i: (n * nt + i, 0)),
                   pl.BlockSpec((1, CONV_HALO, dc), lambda n, i: (n, 0, 0))],
        out_shape=[jax.ShapeDtypeStruct((nb * seq, dc), F32),
                   jax.ShapeDtypeStruct((nb, CONV_HALO, dc), F32)],
        scratch_shapes=[pltpu.VMEM((CONV_HALO + ts, dc), F32)],
        compiler_params=_params("parallel", "arbitrary"),
        name="conv_prompt",
    )(z_main, z_main, conv_buf, conv_w, conv_b, ln_g, ln_b)


def _conv_sample_kernel(glu_ref, buf_ref, cw_ref, cb_ref, g_ref, b_ref, y_ref, new_ref, *, width, dc):
    u = _glu(glu_ref[...], dc)
    acc = cb_ref[...] + u * cw_ref[width - 1:width, :]
    for j in range(width - 1):
        row = buf_ref[j]
        acc = acc + row * cw_ref[j:j + 1, :]
        if j >= 1:
            new_ref[j - 1] = row
    new_ref[width - 2] = u
    y_ref[...] = _silu(_layer_norm(acc, g_ref[...], b_ref[...]))


def _conv_sample(z_main, buf_t, conv_w, conv_b, ln_g, ln_b):
    width, dc = conv_w.shape
    nb = z_main.shape[0]
    tb = _pick_tile(nb, (32, 16, 8))
    kern = functools.partial(_conv_sample_kernel, width=width, dc=dc)
    return pl.pallas_call(
        kern,
        grid=(nb // tb,),
        in_specs=[pl.BlockSpec((tb, 2 * dc), lambda i: (i, 0)),
                  pl.BlockSpec((width - 1, tb, dc), lambda i: (0, i, 0)),
                  pl.BlockSpec((width, dc), lambda i: (0, 0)),
                  pl.BlockSpec((1, dc), lambda i: (0, 0)),
                  pl.BlockSpec((1, dc), lambda i: (0, 0)),
                  pl.BlockSpec((1, dc), lambda i: (0, 0))],
        out_specs=[pl.BlockSpec((tb, dc), lambda i: (i, 0)),
                   pl.BlockSpec((width - 1, tb, dc), lambda i: (0, i, 0))],
        out_shape=[jax.ShapeDtypeStruct((nb, dc), F32),
                   jax.ShapeDtypeStruct((width - 1, nb, dc), F32)],
        compiler_params=_params("parallel"),
        name="conv_sample",
    )(z_main, buf_t, conv_w, conv_b, ln_g, ln_b)


def _mem_prompt_kernel(q_ref, mk_ref, mv_ref, o_ref, *, heads, hd):
    scale = hd ** -0.5
    for h in range(heads):
        sl = slice(h * hd, (h + 1) * hd)
        q = (q_ref[:, sl] * scale).astype(BF16)
        s = _dot_nt(q, mk_ref[:, sl].astype(BF16))
        m = jnp.max(s, axis=-1, keepdims=True)
        p = jnp.exp(s - m)
        l = jnp.sum(p, axis=-1, keepdims=True)
        o_ref[:, sl] = _dot(p.astype(BF16), mv_ref[:, sl].astype(BF16)) / l


def _mem_prompt(z_main, q_col, mk, mv, nb, seq, heads):
    w = mk.shape[1]
    n_mem = mk.shape[0] // nb
    tq = _pick_tile(seq, (512, 256, 128))
    nt = seq // tq
    kern = functools.partial(_mem_prompt_kernel, heads=heads, hd=w // heads)
    return pl.pallas_call(
        kern,
        grid=(nb, nt),
        in_specs=[pl.BlockSpec((tq, w), lambda n, i: (n * nt + i, q_col)),
                  pl.BlockSpec((n_mem, w), lambda n, i: (n, 0)),
                  pl.BlockSpec((n_mem, w), lambda n, i: (n, 0))],
        out_specs=pl.BlockSpec((tq, w), lambda n, i: (n * nt + i, 0)),
        out_shape=jax.ShapeDtypeStruct((nb * seq, w), F32),
        compiler_params=_params("parallel", "parallel"),
        name="mem_prompt",
    )(z_main, mk, mv)


MEM_GROUP = 8


def _head_mask(heads, hd, rows=SUBLANES):
    r = lax.broadcasted_iota(I32, (rows, heads * hd), 0)
    c = lax.broadcasted_iota(I32, (rows, heads * hd), 1)
    return (c >= r * hd) & (c < (r + 1) * hd)


def _mem_sample_kernel(q_ref, mk_ref, mv_ref, o_ref, *, heads, hd):
    scale = hd ** -0.5
    hm = _head_mask(heads, hd)
    for g in range(MEM_GROUP):
        q = q_ref[g:g + 1, :] * scale
        q_bd = jnp.where(hm, jnp.broadcast_to(q, hm.shape), 0.0).astype(BF16)
        s = _dot_nt(q_bd, mk_ref[g].astype(BF16))
        m = jnp.max(s, axis=-1, keepdims=True)
        p = jnp.exp(s - m)
        l = jnp.sum(p, axis=-1, keepdims=True)
        o = _dot(p.astype(BF16), mv_ref[g].astype(BF16)) / l
        o_ref[g:g + 1, :] = jnp.sum(jnp.where(hm, o, 0.0), axis=0, keepdims=True)


def _mem_sample(z_main, q_col, mk, mv, heads):
    nb, n_mem, w = mk.shape
    assert nb % MEM_GROUP == 0 and heads <= SUBLANES
    kern = functools.partial(_mem_sample_kernel, heads=heads, hd=w // heads)
    return pl.pallas_call(
        kern,
        grid=(nb // MEM_GROUP,),
        in_specs=[pl.BlockSpec((MEM_GROUP, w), lambda i: (i, q_col)),
                  pl.BlockSpec((MEM_GROUP, n_mem, w), lambda i: (i, 0, 0)),
                  pl.BlockSpec((MEM_GROUP, n_mem, w), lambda i: (i, 0, 0))],
        out_specs=pl.BlockSpec((MEM_GROUP, w), lambda i: (i, 0)),
        out_shape=jax.ShapeDtypeStruct((nb, w), F32),
        compiler_params=_params("parallel"),
        name="mem_sample",
    )(z_main, mk, mv)


def _topk_select(count, shape, top):
    topf = float(top)

    def bit_step(s, thr):
        cand = jnp.where(s == 0, 0, thr | lax.shift_left(jnp.int32(1), 31 - s))
        cnt = count(lambda k, idx: k >= cand)
        return jnp.where(cnt >= topf, cand, thr)

    thr = lax.fori_loop(0, 32, bit_step, jnp.full(shape, INT_MIN, I32))
    thr = jnp.maximum(thr, KEY_NEG_INF + 1)
    n_gt = count(lambda k, idx: k > thr)
    n_ge = count(lambda k, idx: k >= thr)
    return thr, n_ge, topf - n_gt


def _tie_bound(count, shape, thr, need, n_ge, top, idx_bits):
    full = jnp.full(shape, (1 << idx_bits) - 1, I32)

    def search():
        def bit_step(s, pos):
            cand = pos + lax.shift_left(jnp.int32(1), idx_bits - 1 - s)
            cnt = count(lambda k, idx: (k == thr) & (idx < cand))
            return jnp.where(cnt < need, cand, pos)
        return lax.fori_loop(0, idx_bits, bit_step, jnp.zeros(shape, I32))

    excess = jnp.max(n_ge) > float(top)
    return lax.cond(excess, search, lambda: full)


DSA_BLOCK = 256
SEARCH_BLOCKS = 2
ATTEND_ORDER = ((0, 0), (1, 0), (0, 1), (2, 0), (1, 1), (2, 1))
ATTEND_UNROLL = 4


I16 = jnp.int16
I16_MIN = -(2 ** 15)
HI_NEG_INF = KEY_NEG_INF >> 16
LO_NEG_INF = ((KEY_NEG_INF & 0xFFFF) ^ 0x8000) - 0x10000 * (((KEY_NEG_INF & 0xFFFF) ^ 0x8000) >> 15)
PACKED_ROWS = 2 * SUBLANES


def _search16(count_ge, need, shape):
    def step(s, t):
        cand = jnp.where(s == 0, 0, t | lax.shift_left(jnp.int32(1), 15 - s))
        return jnp.where(count_ge(cand.astype(I16)) >= need, cand, t)
    return lax.fori_loop(0, 16, step, jnp.full(shape, I16_MIN, I32))


def _dsa_prompt_kernel(qt_ref, q6t_ref, wt_ref, k6_ref, k_ref, vt_ref, bias_ref, o_ref,
                       mask_ref, hi_ref, lo_ref, qm_ref, m_ref, acc_ref, s_ref, p_ref, a_ref,
                       *, heads, hd, idx_heads, top, seq):
    blk = DSA_BLOCK
    i = pl.program_id(1)
    pair = LANES // hd
    n_search = (i + SEARCH_BLOCKS) // SEARCH_BLOCKS
    krow = lax.broadcasted_iota(I32, (blk, blk), 0)
    qpos = i * blk + lax.broadcasted_iota(I32, (blk, blk), 1)
    shape = (1, blk)

    def score_block(j, causal):
        kc = k6_ref[pl.ds(pl.multiple_of(j * blk, blk), blk), :]
        sc = jnp.zeros((blk, blk), F32)
        for h in range(idx_heads):
            sc = sc + wt_ref[0, h:h + 1, :] * jnp.maximum(_dot(kc, q6t_ref[0, h]), 0.0)
        if causal:
            sc = jnp.where(j * blk + krow <= qpos, sc, -jnp.inf)
        key = _float_key(sc)
        hi_ref[j] = lax.shift_right_arithmetic(key, 16).astype(I16)
        lo_ref[j] = (key ^ 0x8000).astype(I16)

    def score_blocks(c, _):
        for b in range(SEARCH_BLOCKS):
            score_block(c * SEARCH_BLOCKS + b, False)
        return 0

    lax.fori_loop(0, n_search - 1, score_blocks, 0)
    for b in range(SEARCH_BLOCKS):
        score_block((n_search - 1) * SEARCH_BLOCKS + b, True)

    def idx16(j):
        return (j * blk + krow).astype(I16)

    def count(pred):
        def body(c, acc):
            for b in range(SEARCH_BLOCKS):
                j = c * SEARCH_BLOCKS + b
                ind = jnp.where(pred(hi_ref[j], lo_ref[j], j), jnp.ones((), I16), jnp.zeros((), I16))
                for r in range(blk // PACKED_ROWS):
                    acc = acc + ind[r * PACKED_ROWS:(r + 1) * PACKED_ROWS, :]
            return acc
        acc = lax.fori_loop(0, n_search, body, jnp.zeros((PACKED_ROWS, blk), I16))
        return jnp.sum(acc.astype(I32), axis=0, keepdims=True)

    hi_t = _search16(lambda c: count(lambda hi, lo, j: hi >= c), top, shape)
    hi_t16 = hi_t.astype(I16)
    n_above = count(lambda hi, lo, j: hi > hi_t16)
    n_band = count(lambda hi, lo, j: hi == hi_t16)

    def band_block(j, _):
        lo_ref[j] = jnp.where(hi_ref[j] == hi_t16, lo_ref[j], jnp.full((), I16_MIN, I16))
        return 0

    lax.fori_loop(0, n_search * SEARCH_BLOCKS, band_block, 0)
    lo_t = _search16(lambda c: count(lambda hi, lo, j: lo >= c), top - n_above, shape)
    lo_t16 = lo_t.astype(I16)
    n_gt = n_above + count(lambda hi, lo, j: lo > lo_t16)
    n_ge = n_above + jnp.where(lo_t == I16_MIN, n_band, count(lambda hi, lo, j: lo >= lo_t16))
    need = top - n_gt
    short = hi_t <= HI_NEG_INF
    hi_t16 = jnp.where(short, HI_NEG_INF, hi_t).astype(I16)
    lo_t16 = jnp.where(short, LO_NEG_INF, lo_t).astype(I16)
    n_ge = jnp.where(short, 0, n_ge)

    idx_bits = (seq - 1).bit_length()

    def tie_search():
        def step(s, pos):
            cand = pos + lax.shift_left(jnp.int32(1), idx_bits - 1 - s)
            c16 = cand.astype(I16)
            cnt = count(lambda hi, lo, j: (hi == hi_t16) & (lo == lo_t16) & (idx16(j) < c16))
            return jnp.where(cnt < need, cand, pos)
        return lax.fori_loop(0, idx_bits, step, jnp.zeros(shape, I32))

    pos = lax.cond(jnp.max(n_ge) > top, tie_search, lambda: jnp.full(shape, (1 << idx_bits) - 1, I32))
    pos16 = jnp.where(short, -1, pos).astype(I16)

    def mask_block(j, _):
        hi, lo = hi_ref[j], lo_ref[j]
        tie = (lo == lo_t16) & (idx16(j) <= pos16)
        sel = (hi > hi_t16) | ((hi == hi_t16) & ((lo > lo_t16) | tie))
        mask_ref[j] = jnp.where(sel, jnp.zeros((), BF16), jnp.full((), NEG_INF, BF16)).astype(F32)
        return 0

    lax.fori_loop(0, i + 1, mask_block, 0)

    sub = lax.broadcasted_iota(I32, (LANES, blk), 0)
    for h in range(heads):
        g = h // pair
        own = (sub >= (h % pair) * hd) & (sub < (h % pair + 1) * hd)
        qm_ref[h] = jnp.where(own, qt_ref[0, g * LANES:(g + 1) * LANES, :], jnp.zeros((), BF16))

    m_ref[...] = jnp.full(m_ref.shape, NEG_INF, F32)
    acc_ref[...] = jnp.zeros(acc_ref.shape, F32)

    def attend(j, near):
        rows = pl.ds(pl.multiple_of(j * blk, blk), blk)
        addm = mask_ref[j]
        def logits(h):
            g = h // pair
            s = _dot(k_ref[rows, g * LANES:(g + 1) * LANES], qm_ref[h]) + addm
            if near is not None:
                s = s + bias_ref[near, h]
            s_ref[h] = s
            m_old = m_ref[h]
            m_new = jnp.maximum(m_old, jnp.max(s, axis=0, keepdims=True))
            a_ref[h] = jnp.exp2(m_old - m_new)
            m_ref[h] = m_new

        def probs(h):
            p_ref[h] = jnp.exp2(s_ref[h] - m_ref[h]).astype(BF16)

        def values(h):
            acc_ref[h] = a_ref[h] * acc_ref[h] + _dot(vt_ref[0, j, h], p_ref[h])

        for stage, group in ATTEND_ORDER:
            for h in range(group * heads // 2, (group + 1) * heads // 2):
                (logits, probs, values)[stage](h)

    n_far = jnp.maximum(i - 1, 0)

    def far_group(c, _):
        for u in range(ATTEND_UNROLL):
            attend(ATTEND_UNROLL * c + u, None)
        return 0

    def far_single(j, _):
        attend(j, None)
        return 0

    n_group = n_far // ATTEND_UNROLL
    lax.fori_loop(0, n_group, far_group, 0)
    lax.fori_loop(n_group * ATTEND_UNROLL, n_far, far_single, 0)

    @pl.when(i >= 1)
    def _():
        attend(i - 1, 1)

    attend(i, 0)

    for g in range(heads // pair):
        o = jnp.concatenate([acc_ref[g * pair + r, 0:hd, :] / acc_ref[g * pair + r, hd:hd + 1, :]
                             for r in range(pair)], axis=0)
        o_ref[:, g * LANES:(g + 1) * LANES] = o.T


def _dsa_prompt(qt, q6t, wt, k6, k_bf, vt, bias, nb, seq, heads, top):
    blk = DSA_BLOCK
    w = k_bf.shape[1]
    hd = w // heads
    idx_heads, kd = q6t.shape[1], q6t.shape[2]
    vrows = vt.shape[3]
    assert seq % (blk * SEARCH_BLOCKS) == 0 and LANES % hd == 0 and blk >= REL_MAX_DIST
    assert seq <= 2 ** 15 and vrows > hd
    nq = seq // blk
    once = pl.Buffered(1)
    kern = functools.partial(_dsa_prompt_kernel, heads=heads, hd=hd, idx_heads=idx_heads, top=top, seq=seq)
    return pl.pallas_call(
        kern,
        grid=(nb, nq),
        in_specs=[
            pl.BlockSpec((1, w, blk), lambda n, i: (n, 0, i)),
            pl.BlockSpec((1, idx_heads, kd, blk), lambda n, i: (n, 0, 0, i)),
            pl.BlockSpec((1, idx_heads, blk), lambda n, i: (n, 0, i)),
            pl.BlockSpec((seq, kd), lambda n, i: (n, 0), pipeline_mode=once),
            pl.BlockSpec((seq, w), lambda n, i: (n, 0), pipeline_mode=once),
            pl.BlockSpec((1, nq, heads, vrows, blk), lambda n, i: (n, 0, 0, 0, 0), pipeline_mode=once),
            pl.BlockSpec((2, heads, blk, blk), lambda n, i: (0, 0, 0, 0), pipeline_mode=once),
        ],
        out_specs=pl.BlockSpec((blk, w), lambda n, i: (n * nq + i, 0)),
        out_shape=jax.ShapeDtypeStruct((nb * seq, w), F32),
        scratch_shapes=[pltpu.VMEM((nq, blk, blk), F32),
                        pltpu.VMEM((nq, blk, blk), I16),
                        pltpu.VMEM((nq, blk, blk), I16),
                        pltpu.VMEM((heads, LANES, blk), BF16),
                        pltpu.VMEM((heads, 1, blk), F32),
                        pltpu.VMEM((heads, vrows, blk), F32),
                        pltpu.VMEM((heads, blk, blk), F32),
                        pltpu.VMEM((heads, blk, blk), BF16),
                        pltpu.VMEM((heads, 1, blk), F32)],
        compiler_params=_params("parallel", "arbitrary"),
        name="dsa_prompt",
    )(qt, q6t, wt, k6, k_bf, vt, bias)


def _dsa_sample_scores_kernel(pt_ref, q_ref, w_ref, kown_ref, *rest, n_pages, page):
    del pt_ref
    pages, o_ref = rest[:n_pages], rest[n_pages]
    q = q_ref[0]
    qh, qm, ql = _split3(q)
    w = w_ref[0]
    for p in range(n_pages):
        kh, km, kl = _split3(pages[p][0])
        d = (_dot(qh, kh) + _dot(qh, km) + _dot(qm, kh)
             + _dot(qh, kl) + _dot(qm, km) + _dot(ql, kh))
        sc = jnp.sum(w * jnp.maximum(d, 0.0), axis=0, keepdims=True)
        o_ref[0, :, p * page:(p + 1) * page] = sc
    d_own = jnp.sum(q * kown_ref[0], axis=-1, keepdims=True)
    s_own = jnp.sum(w * jnp.maximum(d_own, 0.0), axis=0, keepdims=True)
    lane = lax.broadcasted_iota(I32, (1, LANES), 1)
    o_ref[0, :, n_pages * page:] = jnp.where(lane == 0, s_own, -jnp.inf)


def _dsa_sample_scores(page_table, q_idx, w_idx, k_own, kidx_t):
    nb, n_pages = page_table.shape
    idx_dim, page = kidx_t.shape[1:]
    assert page % LANES == 0
    width = n_pages * page + LANES
    kern = functools.partial(_dsa_sample_scores_kernel, n_pages=n_pages, page=page)
    page_specs = [pl.BlockSpec((1, idx_dim, page), lambda n, pt, p=p: (pt[n * n_pages + p], 0, 0))
                  for p in range(n_pages)]
    grid_spec = pltpu.PrefetchScalarGridSpec(
        num_scalar_prefetch=1,
        grid=(nb,),
        in_specs=[pl.BlockSpec((1,) + q_idx.shape[1:], lambda n, pt: (n, 0, 0)),
                  pl.BlockSpec((1,) + w_idx.shape[1:], lambda n, pt: (n, 0, 0)),
                  pl.BlockSpec((1, 1, idx_dim), lambda n, pt: (n, 0, 0))] + page_specs,
        out_specs=pl.BlockSpec((1, 1, width), lambda n, pt: (n, 0, 0)),
    )
    return pl.pallas_call(
        kern,
        grid_spec=grid_spec,
        out_shape=jax.ShapeDtypeStruct((nb, 1, width), F32),
        compiler_params=_params("arbitrary"),
        name="dsa_sample_scores",
    )(page_table.reshape(-1), q_idx, w_idx, k_own, *([kidx_t] * n_pages))


def _dsa_sample_mask_kernel(s_ref, o_ref, *, top, width):
    keys = _float_key(s_ref[...])
    idx = lax.broadcasted_iota(I32, keys.shape, 1)

    def count(pred):
        return jnp.sum(jnp.where(pred(keys, idx), 1.0, 0.0), axis=-1, keepdims=True)

    shape = (keys.shape[0], 1)
    thr, n_ge, need = _topk_select(count, shape, top)
    pos = _tie_bound(count, shape, thr, need, n_ge, top, (width - 1).bit_length())
    sel = (keys > thr) | ((keys == thr) & (idx <= pos))
    o_ref[...] = jnp.where(sel, 0.0, NEG_INF)


def _dsa_sample_mask(scores, top):
    rows, width = scores.shape
    return pl.pallas_call(
        functools.partial(_dsa_sample_mask_kernel, top=top, width=width),
        out_shape=jax.ShapeDtypeStruct((rows, width), F32),
        compiler_params=pltpu.CompilerParams(vmem_limit_bytes=VMEM_LIMIT),
        name="dsa_sample_mask",
    )(scores)


def _dsa_sample_attend_kernel(pt_ref, q_ref, kown_ref, vown_ref, mask_ref, bias_ref, *rest,
                              n_pages, page, heads, hd):
    del pt_ref
    kp, vp, o_ref = rest[:n_pages], rest[n_pages:2 * n_pages], rest[2 * n_pages]
    past = n_pages * page
    hm = _head_mask(heads, hd)
    q_bd = jnp.where(hm, jnp.broadcast_to(q_ref[0] * (hd ** -0.5), hm.shape), 0.0)
    q_bf = q_bd.astype(BF16)
    logits = []
    for p in range(n_pages):
        sl = slice(p * page, (p + 1) * page)
        logits.append(_dot(q_bf, kp[p][0].astype(BF16)) + bias_ref[:, sl] + mask_ref[0, :, sl])
    s_own = (jnp.sum(q_bd * kown_ref[0], axis=-1, keepdims=True)
             + bias_ref[:, past:past + 1] + mask_ref[0, :, past:past + 1])
    m = s_own
    for s in logits:
        m = jnp.maximum(m, jnp.max(s, axis=-1, keepdims=True))
    p_own = jnp.exp(s_own - m)
    l = p_own
    acc = p_own * vown_ref[0]
    for p in range(n_pages):
        pr = jnp.exp(logits[p] - m)
        l = l + jnp.sum(pr, axis=-1, keepdims=True)
        acc = acc + _dot_nt(pr.astype(BF16), vp[p][0].astype(BF16))
    o_ref[0] = jnp.sum(jnp.where(hm, acc / l, 0.0), axis=0, keepdims=True)


def _dsa_sample_attend(page_table, q, k_own, v_own, mask, bias, k_t, v_t, heads):
    nb, n_pages = page_table.shape
    w, page = k_t.shape[1:]
    assert heads == SUBLANES
    width = mask.shape[-1]
    kern = functools.partial(_dsa_sample_attend_kernel, n_pages=n_pages, page=page, heads=heads, hd=w // heads)
    page_specs = [pl.BlockSpec((1, w, page), lambda n, pt, p=p: (pt[n * n_pages + p], 0, 0))
                  for p in range(n_pages)]
    row = pl.BlockSpec((1, 1, w), lambda n, pt: (n, 0, 0))
    grid_spec = pltpu.PrefetchScalarGridSpec(
        num_scalar_prefetch=1,
        grid=(nb,),
        in_specs=[row, row, row,
                  pl.BlockSpec((1, 1, width), lambda n, pt: (n, 0, 0)),
                  pl.BlockSpec((heads, width), lambda n, pt: (0, 0))] + page_specs + page_specs,
        out_specs=pl.BlockSpec((1, 1, w), lambda n, pt: (n, 0, 0)),
    )
    return pl.pallas_call(
        kern,
        grid_spec=grid_spec,
        out_shape=jax.ShapeDtypeStruct((nb, 1, w), F32),
        compiler_params=_params("arbitrary"),
        name="dsa_sample_attend",
    )(page_table.reshape(-1), q, k_own, v_own, mask, bias, *([k_t] * n_pages), *([v_t] * n_pages))


def _merge_kernel(x_ref, y_ref, a_ref, m_ref, g1_ref, g2_ref, g3_ref, wc_ref, wa_ref, wm_ref, wo_ref,
                  lg_ref, lb_ref, o_ref):
    c = _dot(y_ref[...].astype(BF16), wc_ref[...])
    a = _dot(a_ref[...].astype(BF16), wa_ref[...])
    m = _dot(m_ref[...].astype(BF16), wm_ref[...])
    merged = (jax.nn.sigmoid(g1_ref[...]) * c + jax.nn.sigmoid(g2_ref[...]) * a
              + jax.nn.sigmoid(g3_ref[...]) * m)
    h = _dot(merged.astype(BF16), wo_ref[...])
    o_ref[...] = _layer_norm(ALPHA * x_ref[...] + h, lg_ref[...], lb_ref[...])


def _merge(x, y_act, att, mem, z_main, g_col, wc, wa, wm, wo, ln_g, ln_b):
    n, d = x.shape
    tm = _pick_tile(n, (256, 128, 64, 32, 16, 8))
    row = lambda width: pl.BlockSpec((tm, width), lambda i: (i, 0))
    gate = lambda b: pl.BlockSpec((tm, d), lambda i: (i, g_col + b))
    full = lambda a: pl.BlockSpec(a.shape, lambda i: (0, 0))
    return pl.pallas_call(
        _merge_kernel,
        grid=(n // tm,),
        in_specs=[row(d), row(y_act.shape[1]), row(att.shape[1]), row(mem.shape[1]),
                  gate(0), gate(1), gate(2), full(wc), full(wa), full(wm), full(wo), full(ln_g), full(ln_b)],
        out_specs=row(d),
        out_shape=jax.ShapeDtypeStruct((n, d), F32),
        compiler_params=_params("parallel"),
        name="merge_ln1",
    )(x, y_act, att, mem, z_main, z_main, z_main, wc, wa, wm, wo, ln_g, ln_b)


MOE_BLOCK = 256
MOE_TOKENS = 128
DMA_PRIORITIES = 2


U32 = jnp.uint32
HIGH_HALF = 0xFFFF0000


def _pack_halves(x):
    c = x.shape[1] // 2
    lo = lax.bitcast_convert_type(x[:, :c].astype(BF16).astype(F32), U32) >> 16
    hi = lax.bitcast_convert_type(x[:, c:].astype(BF16).astype(F32), U32) & jnp.uint32(HIGH_HALF)
    return lo | hi


def _unpack_halves(u):
    return (lax.bitcast_convert_type(u << 16, F32),
            lax.bitcast_convert_type(u & jnp.uint32(HIGH_HALF), F32))


def _router_kernel(x_ref, wh_ref, wl_ref, rb_ref, e_ref, g_ref, r_ref, cnt_ref, xp_ref, carry_ref, *, topk):
    i = pl.program_id(0)

    @pl.when(i == 0)
    def _():
        carry_ref[...] = jnp.zeros(carry_ref.shape, F32)

    xp_ref[...] = _pack_halves(x_ref[...])
    xh, xl = _split2(x_ref[...])
    logits = _dot(xh, wh_ref[...]) + _dot(xl, wh_ref[...]) + _dot(xh, wl_ref[...])
    scores = jax.nn.sigmoid(logits)
    tm, ne = scores.shape
    sel = scores + rb_ref[...]
    lane = lax.broadcasted_iota(I32, (tm, ne), 1).astype(F32)
    chosen = jnp.zeros((tm, ne), F32)
    picks = []
    total = jnp.zeros((tm, 1), F32)
    for _ in range(topk):
        mx = jnp.max(sel, axis=-1, keepdims=True)
        idx = jnp.min(jnp.where(sel == mx, lane, float(ne)), axis=-1, keepdims=True)
        hot = lane == idx
        sk = jnp.sum(jnp.where(hot, scores, 0.0), axis=-1, keepdims=True)
        sel = jnp.where(hot, -jnp.inf, sel)
        chosen = jnp.where(hot, 1.0, chosen)
        total = total + sk
        picks.append((idx, sk, hot))
    r = lax.broadcasted_iota(I32, (tm, tm), 0)
    c = lax.broadcasted_iota(I32, (tm, tm), 1)
    before = jnp.where(c < r, 1.0, 0.0).astype(BF16)
    rank = _dot(before, chosen.astype(BF16)) + carry_ref[...]
    out_lane = lax.broadcasted_iota(I32, (tm, LANES), 1)
    e_out = jnp.zeros((tm, LANES), I32)
    g_out = jnp.zeros((tm, LANES), F32)
    r_out = jnp.zeros((tm, LANES), I32)
    for k, (idx, sk, hot) in enumerate(picks):
        rk = jnp.sum(jnp.where(hot, rank, 0.0), axis=-1, keepdims=True)
        e_out = jnp.where(out_lane == k, idx.astype(I32), e_out)
        g_out = jnp.where(out_lane == k, sk / total * ROUTED_SCALE, g_out)
        r_out = jnp.where(out_lane == k, rk.astype(I32), r_out)
    e_ref[...] = e_out
    g_ref[...] = g_out
    r_ref[...] = r_out
    carry_ref[...] = carry_ref[...] + jnp.sum(chosen, axis=0, keepdims=True)
    cnt_ref[...] = carry_ref[...]


def _router(x1, w_router, router_bias):
    n, d = x1.shape
    ne = w_router.shape[1]
    tm = _pick_tile(n, (256, 128, 64, 48, 32, 16, 8))
    wh, wl = _split2(w_router)
    out = lambda: pl.BlockSpec((tm, LANES), lambda i: (i, 0))
    return pl.pallas_call(
        functools.partial(_router_kernel, topk=TOP_K),
        grid=(n // tm,),
        in_specs=[pl.BlockSpec((tm, d), lambda i: (i, 0)),
                  pl.BlockSpec((d, ne), lambda i: (0, 0)),
                  pl.BlockSpec((d, ne), lambda i: (0, 0)),
                  pl.BlockSpec((1, ne), lambda i: (0, 0))],
        out_specs=[out(), out(), out(), pl.BlockSpec((1, ne), lambda i: (0, 0)),
                   pl.BlockSpec((tm, d // 2), lambda i: (i, 0))],
        out_shape=[jax.ShapeDtypeStruct((n, LANES), I32), jax.ShapeDtypeStruct((n, LANES), F32),
                   jax.ShapeDtypeStruct((n, LANES), I32), jax.ShapeDtypeStruct((1, ne), F32),
                   jax.ShapeDtypeStruct((n, d // 2), U32)],
        scratch_shapes=[pltpu.VMEM((1, ne), F32)],
        compiler_params=_params("arbitrary"),
        name="router",
    )(x1, wh, wl, router_bias)


def _slots_kernel(e_ref, r_ref, ps_ref, o_ref, *, topk):
    e, r = e_ref[...], r_ref[...]
    tm, ne = e.shape[0], ps_ref.shape[1]
    lane = lax.broadcasted_iota(I32, (tm, ne), 1)
    out_lane = lax.broadcasted_iota(I32, e.shape, 1)
    out = jnp.zeros(e.shape, I32)
    for k in range(topk):
        start = jnp.sum(jnp.where(lane == e[:, k:k + 1], ps_ref[...], 0.0), axis=-1, keepdims=True)
        out = jnp.where(out_lane == k, start.astype(I32) + r, out)
    o_ref[...] = out


def _slots(top_e, rank, pstart):
    n = top_e.shape[0]
    tm = _pick_tile(n, (256, 128, 64, 48, 32, 16, 8))
    row = pl.BlockSpec((tm, LANES), lambda i: (i, 0))
    return pl.pallas_call(
        functools.partial(_slots_kernel, topk=TOP_K),
        grid=(n // tm,),
        in_specs=[row, row, pl.BlockSpec(pstart.shape, lambda i: (0, 0))],
        out_specs=row,
        out_shape=jax.ShapeDtypeStruct((n, LANES), I32),
        compiler_params=_params("parallel"),
        name="moe_slots",
    )(top_e, rank, pstart)


def _dispatch_kernel(dest_ref, x_ref, init_ref, xs_ref, sem, *, tt, topk):
    del init_ref

    def copy(t, d):
        return pltpu.make_async_copy(x_ref.at[pl.ds(t, 1)], xs_ref.at[pl.ds(d, 1)], sem)

    def start(t, _):
        for k in range(topk):
            copy(t, dest_ref[t * topk + k]).start(priority=k % DMA_PRIORITIES)
        return 0

    lax.fori_loop(0, tt, start, 0, unroll=4)

    def wait(t, _):
        for k in range(topk):
            copy(t, dest_ref[t * topk + k]).wait()
        return 0

    lax.fori_loop(0, tt, wait, 0)


def _dispatch(x1, dest, cap):
    n, d = x1.shape
    topk = dest.shape[1]
    tt = _pick_tile(n, (MOE_TOKENS, 64, 48, 32, 16, 8))
    init = jnp.zeros((cap, d), x1.dtype)
    return pl.pallas_call(
        functools.partial(_dispatch_kernel, tt=tt, topk=topk),
        grid=(n // tt,),
        in_specs=[pl.BlockSpec((tt * topk,), lambda i: (i,), memory_space=pltpu.SMEM),
                  pl.BlockSpec((tt, d), lambda i: (i, 0)),
                  pl.BlockSpec(memory_space=pl.ANY)],
        out_specs=pl.BlockSpec(memory_space=pl.ANY),
        out_shape=jax.ShapeDtypeStruct((cap, d), x1.dtype),
        scratch_shapes=[pltpu.SemaphoreType.DMA(())],
        input_output_aliases={2: 0},
        compiler_params=_params("arbitrary"),
        name="moe_dispatch",
    )(dest.reshape(-1), x1, init)


def _expert_kernel(be_ref, na_ref, xs_ref, wgu_ref, wdn_ref, y_ref, wgu_bf, wdn_bf, *, de):
    b = pl.program_id(0)

    @pl.when(b < na_ref[0])
    def _():
        @pl.when((b == 0) | (be_ref[b] != be_ref[jnp.maximum(b - 1, 0)]))
        def _():
            wgu_bf[...] = wgu_ref[0].astype(BF16)
            wdn_bf[...] = wdn_ref[0].astype(BF16)

        lo, hi = _unpack_halves(xs_ref[...])
        half = lo.shape[1]
        h = _dot(lo.astype(BF16), wgu_bf[0:half, :]) + _dot(hi.astype(BF16), wgu_bf[half:, :])
        act = _silu(h[:, :de]) * h[:, de:]
        y_ref[...] = _pack_halves(_dot(act.astype(BF16), wdn_bf[...]))

    @pl.when(b >= na_ref[0])
    def _():
        y_ref[...] = jnp.zeros(y_ref.shape, U32)


def _experts(xs, blk_exp, n_active, w_gu, w_down):
    cap, dp = xs.shape
    ne, d, de2 = w_gu.shape
    de = de2 // 2
    n_blk = cap // MOE_BLOCK
    grid_spec = pltpu.PrefetchScalarGridSpec(
        num_scalar_prefetch=2,
        grid=(n_blk,),
        in_specs=[pl.BlockSpec((MOE_BLOCK, dp), lambda b, be, na: (b, 0)),
                  pl.BlockSpec((1, d, de2), lambda b, be, na: (be[b], 0, 0)),
                  pl.BlockSpec((1, de, d), lambda b, be, na: (be[b], 0, 0))],
        out_specs=pl.BlockSpec((MOE_BLOCK, dp), lambda b, be, na: (b, 0)),
        scratch_shapes=[pltpu.VMEM((d, de2), BF16), pltpu.VMEM((de, d), BF16)],
    )
    return pl.pallas_call(
        functools.partial(_expert_kernel, de=de),
        grid_spec=grid_spec,
        out_shape=jax.ShapeDtypeStruct((cap, dp), U32),
        compiler_params=_params("arbitrary"),
        name="moe_experts",
    )(blk_exp, n_active, xs, w_gu, w_down)


def _combine_kernel(dest_ref, gate_ref, x_ref, y_ref, wgu_ref, wdn_ref, lg_ref, lb_ref, o_ref, buf, sem,
                    *, tt, topk, ds):
    def copy(t, k):
        return pltpu.make_async_copy(y_ref.at[pl.ds(dest_ref[t * topk + k], 1)], buf.at[k, pl.ds(t, 1)], sem)

    def start(t, _):
        for k in range(topk):
            copy(t, k).start(priority=k % DMA_PRIORITIES)
        return 0

    lax.fori_loop(0, tt, start, 0, unroll=4)

    x = x_ref[...]
    h = _dot(x.astype(BF16), wgu_ref[...])
    shared = _dot((_silu(h[:, :ds]) * h[:, ds:]).astype(BF16), wdn_ref[...])

    def wait(t, _):
        for k in range(topk):
            copy(t, k).wait()
        return 0

    lax.fori_loop(0, tt, wait, 0)

    r_lo = jnp.zeros(buf.shape[1:], F32)
    r_hi = jnp.zeros(buf.shape[1:], F32)
    for k in range(topk):
        lo, hi = _unpack_halves(buf[k])
        r_lo = r_lo + gate_ref[:, k:k + 1] * lo
        r_hi = r_hi + gate_ref[:, k:k + 1] * hi
    routed = jnp.concatenate([r_lo, r_hi], axis=-1)
    o_ref[...] = _layer_norm(ALPHA * x + routed + shared, lg_ref[...], lb_ref[...])


def _combine(x1, y_rows, dest, gates, w_sh_gu, w_sh_down, ln_g, ln_b):
    n, d = x1.shape
    topk = dest.shape[1]
    tt = _pick_tile(n, (MOE_TOKENS, 64, 48, 32, 16, 8))
    full = lambda a: pl.BlockSpec(a.shape, lambda i: (0, 0))
    return pl.pallas_call(
        functools.partial(_combine_kernel, tt=tt, topk=topk, ds=w_sh_down.shape[0]),
        grid=(n // tt,),
        in_specs=[pl.BlockSpec((tt * topk,), lambda i: (i,), memory_space=pltpu.SMEM),
                  pl.BlockSpec((tt, LANES), lambda i: (i, 0)),
                  pl.BlockSpec((tt, d), lambda i: (i, 0)),
                  pl.BlockSpec(memory_space=pl.ANY),
                  full(w_sh_gu), full(w_sh_down), full(ln_g), full(ln_b)],
        out_specs=pl.BlockSpec((tt, d), lambda i: (i, 0)),
        out_shape=jax.ShapeDtypeStruct((n, d), F32),
        scratch_shapes=[pltpu.VMEM((topk, tt, y_rows.shape[1]), U32), pltpu.SemaphoreType.DMA(())],
        compiler_params=_params("arbitrary"),
        name="moe_combine",
    )(dest.reshape(-1), gates, x1, y_rows, w_sh_gu, w_sh_down, ln_g, ln_b)


def _moe_ln2(x1, w_router, router_bias, w_exp_gu, w_exp_down, w_sh_gu, w_sh_down, ln_g, ln_b):
    n, d = x1.shape
    ne = w_router.shape[1]
    top_e, gates, rank, counts, x1_packed = _router(x1, w_router, router_bias)
    counts = counts[0].astype(I32)
    padded = (counts + MOE_BLOCK - 1) // MOE_BLOCK * MOE_BLOCK
    pend = jnp.cumsum(padded)
    pstart = pend - padded
    dest = _slots(top_e, rank, pstart.astype(F32)[None, :])[:, :TOP_K]
    cap = (n * TOP_K + ne * (MOE_BLOCK - 1) + MOE_BLOCK - 1) // MOE_BLOCK * MOE_BLOCK
    n_blk = cap // MOE_BLOCK
    blk_start = jnp.arange(n_blk, dtype=I32) * MOE_BLOCK
    blk_exp = jnp.minimum(jnp.sum((pend[None, :] <= blk_start[:, None]).astype(I32), axis=1), ne - 1)
    n_active = (pend[-1:] // MOE_BLOCK).astype(I32)
    xs = _dispatch(x1_packed, dest, cap)
    y_rows = _experts(xs, blk_exp, n_active, w_exp_gu, w_exp_down)
    return _combine(x1, y_rows, dest, gates, w_sh_gu.astype(BF16), w_sh_down.astype(BF16), ln_g, ln_b)


def _rel_bucket(dist):
    n = jnp.maximum(dist, 0)
    max_exact = N_BUCKETS // 2
    nf = jnp.maximum(n, max_exact).astype(F32)
    large = max_exact + (jnp.log(nf / max_exact) / math.log(REL_MAX_DIST / max_exact)
                         * (N_BUCKETS - max_exact)).astype(I32)
    return jnp.where(n < max_exact, n, jnp.minimum(large, N_BUCKETS - 1))


def _pages_t(cache):
    n_pool, page = cache.shape[:2]
    return jnp.moveaxis(cache, 1, -1).reshape(n_pool, -1, page)


def _bias_table(rel_bias, dist):
    hit = _rel_bucket(dist)[None, :, None] == jnp.arange(N_BUCKETS, dtype=I32)[None, None, :]
    return jnp.sum(jnp.where(hit, rel_bias.T[:, None, :], 0.0), axis=-1)


def kernel(x_prompt, x_sample, mem_prompt, cache_k, cache_v, cache_kidx, cache_mem_k, cache_mem_v, state_conv, page_table, rel_bias, w_in, b_in, conv_w, conv_b, conv_ln_g, conv_ln_b, w_conv_out, w_attn_out, w_mem_kv, w_mem_out, w_out, ln1_g, ln1_b, w_router, router_bias, w_exp_gu, w_exp_down, w_sh_gu, w_sh_down, ln2_g, ln2_b):
    assert w_in.shape[0] == DEPTH == 1
    nb, seq, d = x_prompt.shape
    nsb, nst, _ = x_sample.shape
    assert nst == 1
    width, dc = conv_w.shape[1:]
    _, n_pool, page, heads, hd = cache_k.shape
    idx_dim = cache_kidx.shape[-1]
    n_mem, mem_heads, mem_hd = cache_mem_k.shape[2:]
    aw, mw, iw = heads * hd, mem_heads * mem_hd, N_IDX_HEADS * idx_dim
    assert 2 * dc == d and aw * 2 == d and mw * 2 == d

    o_q = 2 * dc
    o_qi = o_q + 3 * aw
    o_ki = o_qi + iw
    o_wi = o_ki + idx_dim
    o_qm = o_wi + N_IDX_HEADS
    o_g = o_qm + mw
    w0, b0 = w_in[0], b_in[0]
    w_main = jnp.concatenate([w0[:, :o_qi], w0[:, o_qm:]], axis=1).astype(BF16)
    b_main = jnp.concatenate([b0[:o_qi], b0[o_qm:]])[None, :]
    n_idx = o_qm - o_qi
    n_idx_pad = -(-n_idx // LANES) * LANES
    w_idx = jnp.pad(w0[:, o_qi:o_qm], ((0, 0), (0, n_idx_pad - n_idx)))
    b_idx = jnp.pad(b0[o_qi:o_qm], (0, n_idx_pad - n_idx))[None, :]
    col_q, col_k, col_v, col_qm, col_g = 2, 3, 4, 5, 3

    xp = x_prompt.reshape(nb * seq, d)
    xs = x_sample.reshape(nsb, d)
    zp = _linear(xp, w_main, b_main, tn=1024)
    zs = _linear(xs, w_main, b_main, tn=1024)
    zip_ = _linear(xp, w_idx, b_idx, split=True, tn=n_idx_pad)
    zis = _linear(xs, w_idx, b_idx, split=True, tn=n_idx_pad)

    kp, vp = zp[:, col_k * aw:(col_k + 1) * aw], zp[:, col_v * aw:(col_v + 1) * aw]
    ks, vs = zs[:, col_k * aw:(col_k + 1) * aw], zs[:, col_v * aw:(col_v + 1) * aw]
    kip, kis = zip_[:, iw:iw + idx_dim], zis[:, iw:iw + idx_dim]

    conv0 = jnp.zeros((nb, width - 1, dc), F32)
    cw, cb, cg, cbb = conv_w[0], conv_b[0][None, :], conv_ln_g[0][None, :], conv_ln_b[0][None, :]
    yp, tail_p = _conv_prompt(zp, conv0, cw, cb, cg, cbb, nb, seq)
    conv_p = tail_p[:, CONV_HALO - (width - 1):, :]
    ys, new_t = _conv_sample(zs, jnp.swapaxes(state_conv[0], 0, 1), cw, cb, cg, cbb)
    conv_s = jnp.swapaxes(new_t, 0, 1)

    wkv = w_mem_kv[0].astype(BF16)
    mkv = _linear(mem_prompt.reshape(nb * n_mem, d), wkv, jnp.zeros((1, 2 * mw), F32))
    mk_p, mv_p = mkv[:, :mw], mkv[:, mw:]
    mem_p = _mem_prompt(zp, col_qm, mk_p, mv_p, nb, seq, mem_heads)
    mem_s = _mem_sample(zs, col_qm, cache_mem_k[0].reshape(nsb, n_mem, mw),
                        cache_mem_v[0].reshape(nsb, n_mem, mw), mem_heads)

    far = _bias_table(rel_bias, jnp.full((1,), 4 * REL_MAX_DIST, I32))
    a = jnp.arange(DSA_BLOCK, dtype=I32)
    dist = jnp.stack([a[None, :] - a[:, None], DSA_BLOCK + a[None, :] - a[:, None]])
    bias_p = (_bias_table(rel_bias, dist.reshape(-1)) - far).reshape(heads, 2, DSA_BLOCK, DSA_BLOCK)
    bias_p = jnp.swapaxes(bias_p, 0, 1) * LOG2E
    qh, qm_, ql = _split3(zip_[:, :iw].reshape(nb, seq, N_IDX_HEADS, idx_dim))
    q6t = jnp.transpose(jnp.concatenate([qh, qh, qm_, qh, qm_, ql], axis=-1), (0, 2, 3, 1))
    kh, km, kl = _split3(kip)
    k6 = jnp.concatenate([kh, km, kh, kl, km, kh], axis=-1)
    wt = jnp.swapaxes(zip_[:, iw + idx_dim:iw + idx_dim + N_IDX_HEADS].reshape(nb, seq, N_IDX_HEADS), 1, 2)
    qt = jnp.swapaxes((zp[:, col_q * aw:(col_q + 1) * aw] * (hd ** -0.5 * LOG2E)).astype(BF16)
                      .reshape(nb, seq, aw), 1, 2)
    nkb = seq // DSA_BLOCK
    vt = jnp.transpose(vp.astype(BF16).reshape(nb, nkb, DSA_BLOCK, heads, hd), (0, 1, 3, 4, 2))
    vt = jnp.concatenate([vt, jnp.ones((nb, nkb, heads, 1, DSA_BLOCK), BF16),
                          jnp.zeros((nb, nkb, heads, PACKED_ROWS - 1, DSA_BLOCK), BF16)], axis=3)
    top_p = min(TOPK_MAX, seq // 4)
    att_p = _dsa_prompt(qt, q6t, wt, k6, kp.astype(BF16), vt, bias_p, nb, seq, heads, top_p)

    n_pages = page_table.shape[1]
    past = n_pages * page
    top_s = min(TOPK_MAX, (past + 1) // 4)
    scores = _dsa_sample_scores(page_table, zis[:, :iw].reshape(nsb, N_IDX_HEADS, idx_dim),
                                zis[:, iw + idx_dim:iw + idx_dim + N_IDX_HEADS].reshape(nsb, N_IDX_HEADS, 1),
                                kis.reshape(nsb, 1, idx_dim), _pages_t(cache_kidx[0]))
    mask_s = _dsa_sample_mask(scores.reshape(nsb, past + LANES), top_s)
    spos = jnp.arange(past + LANES, dtype=I32)
    bias_s = _bias_table(rel_bias, past - spos)
    att_s = _dsa_sample_attend(page_table, zs[:, col_q * aw:(col_q + 1) * aw].reshape(nsb, 1, aw),
                               ks.reshape(nsb, 1, aw), vs.reshape(nsb, 1, aw),
                               mask_s.reshape(nsb, 1, past + LANES), bias_s,
                               _pages_t(cache_k[0]), _pages_t(cache_v[0]), heads).reshape(nsb, aw)

    wc, wa, wm, wo = (w_conv_out[0].astype(BF16), w_attn_out[0].astype(BF16),
                      w_mem_out[0].astype(BF16), w_out[0].astype(BF16))
    l1g, l1b = ln1_g[0][None, :], ln1_b[0][None, :]
    x1p = _merge(xp, yp, att_p, mem_p, zp, col_g, wc, wa, wm, wo, l1g, l1b)
    x1s = _merge(xs, ys, att_s, mem_s, zs, col_g, wc, wa, wm, wo, l1g, l1b)
    x1 = jnp.concatenate([x1p, x1s], axis=0)
    x2 = _moe_ln2(x1, w_router[0], router_bias[0][None, :], w_exp_gu[0], w_exp_down[0],
                  w_sh_gu[0], w_sh_down[0], ln2_g[0][None, :], ln2_b[0][None, :])

    return (x2[:nb * seq].reshape(nb, seq, d), x2[nb * seq:].reshape(nsb, 1, d),
            kp.reshape(1, nb, seq, heads, hd), vp.reshape(1, nb, seq, heads, hd),
            kip.reshape(1, nb, seq, idx_dim),
            mk_p.reshape(1, nb, n_mem, mem_heads, mem_hd), mv_p.reshape(1, nb, n_mem, mem_heads, mem_hd),
            conv_p[None],
            ks.reshape(1, nsb, 1, heads, hd), vs.reshape(1, nsb, 1, heads, hd),
            kis.reshape(1, nsb, 1, idx_dim), conv_s[None])
```

```python
import functools
import math

import jax
import jax.numpy as jnp
from jax import lax
from jax.experimental import pallas as pl
from jax.experimental.pallas import tpu as pltpu

F32 = jnp.float32
BF16 = jnp.bfloat16
I32 = jnp.int32

DEPTH = 1
N_IDX_HEADS = 8
TOPK_MAX = 256
N_BUCKETS = 32
REL_MAX_DIST = 128
TOP_K = 8
ROUTED_SCALE = 2.5
ALPHA = (2 * DEPTH) ** 0.25
LN_EPS = 1e-5
NEG_INF = -1e30
LOG2E = math.log2(math.e)

LANES = 128
SUBLANES = 8
VMEM_LIMIT = 56 * 1024 * 1024

INT_MIN = -(2 ** 31)
KEY_NEG_INF = -2139095041

_dot = functools.partial(jnp.dot, preferred_element_type=F32)


def _dot_nt(a, b):
    return lax.dot_general(a, b, (((1,), (1,)), ((), ())), preferred_element_type=F32)


def _pick_tile(n, cands=(1024, 512, 384, 256, 128, 64, 48, 32, 16, 8)):
    for c in cands:
        if n % c == 0:
            return c
    raise ValueError(f"no row tile divides {n}")


def _params(*sem):
    return pltpu.CompilerParams(dimension_semantics=sem, vmem_limit_bytes=VMEM_LIMIT)


def _trunc_bf16(x):
    bits = lax.bitcast_convert_type(x, jnp.uint32) & jnp.uint32(0xFFFF0000)
    return lax.bitcast_convert_type(bits, F32)


def _split2(x):
    hi = _trunc_bf16(x)
    return hi.astype(BF16), (x - hi).astype(BF16)


def _split3(x):
    hi = _trunc_bf16(x)
    r = x - hi
    mid = _trunc_bf16(r)
    lo = r - mid
    return hi.astype(BF16), mid.astype(BF16), lo.astype(BF16)


def _layer_norm(x, g, b):
    mu = jnp.mean(x, axis=-1, keepdims=True)
    xc = x - mu
    var = jnp.mean(xc * xc, axis=-1, keepdims=True)
    return xc * lax.rsqrt(var + LN_EPS) * g + b


def _silu(x):
    return x * jax.nn.sigmoid(x)


def _float_key(x):
    x = jnp.where(x == 0.0, 0.0, x)
    bits = lax.bitcast_convert_type(x, I32)
    return jnp.where(bits >= 0, bits, bits ^ jnp.int32(0x7FFFFFFF))


def _linear_kernel(x_ref, w_ref, b_ref, o_ref, *, split):
    x = x_ref[...]
    if split:
        xh, xl = _split2(x)
        wh, wl = _split2(w_ref[...])
        acc = _dot(xh, wh) + _dot(xl, wh) + _dot(xh, wl)
    else:
        acc = _dot(x.astype(BF16), w_ref[...])
    o_ref[...] = acc + b_ref[...]


def _linear(x, w, b, *, split=False, tn=512):
    m, k = x.shape
    n = w.shape[1]
    tm = _pick_tile(m)
    tn = min(tn, n)
    assert n % tn == 0
    return pl.pallas_call(
        functools.partial(_linear_kernel, split=split),
        grid=(m // tm, n // tn),
        in_specs=[pl.BlockSpec((tm, k), lambda i, j: (i, 0)),
                  pl.BlockSpec((k, tn), lambda i, j: (0, j)),
                  pl.BlockSpec((1, tn), lambda i, j: (0, j))],
        out_specs=pl.BlockSpec((tm, tn), lambda i, j: (i, j)),
        out_shape=jax.ShapeDtypeStruct((m, n), F32),
        compiler_params=_params("parallel", "parallel"),
        name="linear_split" if split else "linear",
    )(x, w, b)


CONV_HALO = 32
CONV_SUB = 64


def _glu(z, dc):
    return z[:, :dc] * jax.nn.sigmoid(z[:, dc:])


def _conv_prompt_kernel(glu_ref, prev_ref, buf_ref, cw_ref, cb_ref, g_ref, b_ref, y_ref, tail_ref, ext_ref,
                        *, ts, width, dc):
    i = pl.program_id(1)
    u = _glu(glu_ref[...], dc)
    ext_ref[0:CONV_HALO, :] = _glu(prev_ref[...], dc)

    @pl.when(i == 0)
    def _():
        ext_ref[CONV_HALO - (width - 1):CONV_HALO, :] = buf_ref[0]

    ext_ref[CONV_HALO:CONV_HALO + ts, :] = u
    tail_ref[0] = u[ts - CONV_HALO:, :]
    off = CONV_HALO - (width - 1)
    for r in range(ts // CONV_SUB):
        acc = jnp.broadcast_to(cb_ref[...], (CONV_SUB, dc))
        for j in range(width):
            lo = r * CONV_SUB + off + j
            acc = acc + ext_ref[lo:lo + CONV_SUB, :] * cw_ref[j:j + 1, :]
        y_ref[r * CONV_SUB:(r + 1) * CONV_SUB, :] = _silu(_layer_norm(acc, g_ref[...], b_ref[...]))


def _conv_prompt(z_main, conv_buf, conv_w, conv_b, ln_g, ln_b, nb, seq):
    width, dc = conv_w.shape
    assert width - 1 <= CONV_HALO
    ts = 256
    assert seq % ts == 0 and ts % CONV_SUB == 0 and ts % CONV_HALO == 0
    nt = seq // ts
    kern = functools.partial(_conv_prompt_kernel, ts=ts, width=width, dc=dc)
    return pl.pallas_call(
        kern,
        grid=(nb, nt),
        in_specs=[
            pl.BlockSpec((ts, 2 * dc), lambda n, i: (n * nt + i, 0)),
            pl.BlockSpec((CONV_HALO, 2 * dc),
                         lambda n, i: (jnp.maximum((n * nt + i) * (ts // CONV_HALO) - 1, 0), 0)),
            pl.BlockSpec((1, width - 1, dc), lambda n, i: (n, 0, 0)),
            pl.BlockSpec((width, dc), lambda n, i: (0, 0)),
            pl.BlockSpec((1, dc), lambda n, i: (0, 0)),
            pl.BlockSpec((1, dc), lambda n, i: (0, 0)),
            pl.BlockSpec((1, dc), lambda n, i: (0, 0)),
        ],
        out_specs=[pl.BlockSpec((ts, dc), lambda n, i: (n * nt + i, 0)),
                   pl.BlockSpec((1, CONV_HALO, dc), lambda n, i: (n, 0, 0))],
        out_shape=[jax.ShapeDtypeStruct((nb * seq, dc), F32),
                   jax.ShapeDtypeStruct((nb, CONV_HALO, dc), F32)],
        scratch_shapes=[pltpu.VMEM((CONV_HALO + ts, dc), F32)],
        compiler_params=_params("parallel", "arbitrary"),
        name="conv_prompt",
    )(z_main, z_main, conv_buf, conv_w, conv_b, ln_g, ln_b)


def _conv_sample_kernel(glu_ref, buf_ref, cw_ref, cb_ref, g_ref, b_ref, y_ref, new_ref, *, width, dc):
    u = _glu(glu_ref[...], dc)
    acc = cb_ref[...] + u * cw_ref[width - 1:width, :]
    for j in range(width - 1):
        row = buf_ref[j]
        acc = acc + row * cw_ref[j:j + 1, :]
        if j >= 1:
            new_ref[j - 1] = row
    new_ref[width - 2] = u
    y_ref[...] = _silu(_layer_norm(acc, g_ref[...], b_ref[...]))


def _conv_sample(z_main, buf_t, conv_w, conv_b, ln_g, ln_b):
    width, dc = conv_w.shape
    nb = z_main.shape[0]
    tb = _pick_tile(nb, (32, 16, 8))
    kern = functools.partial(_conv_sample_kernel, width=width, dc=dc)
    return pl.pallas_call(
        kern,
        grid=(nb // tb,),
        in_specs=[pl.BlockSpec((tb, 2 * dc), lambda i: (i, 0)),
                  pl.BlockSpec((width - 1, tb, dc), lambda i: (0, i, 0)),
                  pl.BlockSpec((width, dc), lambda i: (0, 0)),
                  pl.BlockSpec((1, dc), lambda i: (0, 0)),
                  pl.BlockSpec((1, dc), lambda i: (0, 0)),
                  pl.BlockSpec((1, dc), lambda i: (0, 0))],
        out_specs=[pl.BlockSpec((tb, dc), lambda i: (i, 0)),
                   pl.BlockSpec((width - 1, tb, dc), lambda i: (0, i, 0))],
        out_shape=[jax.ShapeDtypeStruct((nb, dc), F32),
                   jax.ShapeDtypeStruct((width - 1, nb, dc), F32)],
        compiler_params=_params("parallel"),
        name="conv_sample",
    )(z_main, buf_t, conv_w, conv_b, ln_g, ln_b)


def _mem_prompt_kernel(q_ref, mk_ref, mv_ref, o_ref, *, heads, hd):
    scale = hd ** -0.5
    for h in range(heads):
        sl = slice(h * hd, (h + 1) * hd)
        q = (q_ref[:, sl] * scale).astype(BF16)
        s = _dot_nt(q, mk_ref[:, sl].astype(BF16))
        m = jnp.max(s, axis=-1, keepdims=True)
        p = jnp.exp(s - m)
        l = jnp.sum(p, axis=-1, keepdims=True)
        o_ref[:, sl] = _dot(p.astype(BF16), mv_ref[:, sl].astype(BF16)) / l


def _mem_prompt(z_main, q_col, mk, mv, nb, seq, heads):
    w = mk.shape[1]
    n_mem = mk.shape[0] // nb
    tq = _pick_tile(seq, (512, 256, 128))
    nt = seq // tq
    kern = functools.partial(_mem_prompt_kernel, heads=heads, hd=w // heads)
    return pl.pallas_call(
        kern,
        grid=(nb, nt),
        in_specs=[pl.BlockSpec((tq, w), lambda n, i: (n * nt + i, q_col)),
                  pl.BlockSpec((n_mem, w), lambda n, i: (n, 0)),
                  pl.BlockSpec((n_mem, w), lambda n, i: (n, 0))],
        out_specs=pl.BlockSpec((tq, w), lambda n, i: (n * nt + i, 0)),
        out_shape=jax.ShapeDtypeStruct((nb * seq, w), F32),
        compiler_params=_params("parallel", "parallel"),
        name="mem_prompt",
    )(z_main, mk, mv)


MEM_GROUP = 8


def _head_mask(heads, hd, rows=SUBLANES):
    r = lax.broadcasted_iota(I32, (rows, heads * hd), 0)
    c = lax.broadcasted_iota(I32, (rows, heads * hd), 1)
    return (c >= r * hd) & (c < (r + 1) * hd)


def _mem_sample_kernel(q_ref, mk_ref, mv_ref, o_ref, *, heads, hd):
    scale = hd ** -0.5
    hm = _head_mask(heads, hd)
    for g in range(MEM_GROUP):
        q = q_ref[g:g + 1, :] * scale
        q_bd = jnp.where(hm, jnp.broadcast_to(q, hm.shape), 0.0).astype(BF16)
        s = _dot_nt(q_bd, mk_ref[g].astype(BF16))
        m = jnp.max(s, axis=-1, keepdims=True)
        p = jnp.exp(s - m)
        l = jnp.sum(p, axis=-1, keepdims=True)
        o = _dot(p.astype(BF16), mv_ref[g].astype(BF16)) / l
        o_ref[g:g + 1, :] = jnp.sum(jnp.where(hm, o, 0.0), axis=0, keepdims=True)


def _mem_sample(z_main, q_col, mk, mv, heads):
    nb, n_mem, w = mk.shape
    assert nb % MEM_GROUP == 0 and heads <= SUBLANES
    kern = functools.partial(_mem_sample_kernel, heads=heads, hd=w // heads)
    return pl.pallas_call(
        kern,
        grid=(nb // MEM_GROUP,),
        in_specs=[pl.BlockSpec((MEM_GROUP, w), lambda i: (i, q_col)),
                  pl.BlockSpec((MEM_GROUP, n_mem, w), lambda i: (i, 0, 0)),
                  pl.BlockSpec((MEM_GROUP, n_mem, w), lambda i: (i, 0, 0))],
        out_specs=pl.BlockSpec((MEM_GROUP, w), lambda i: (i, 0)),
        out_shape=jax.ShapeDtypeStruct((nb, w), F32),
        compiler_params=_params("parallel"),
        name="mem_sample",
    )(z_main, mk, mv)


def _topk_select(count, shape, top):
    topf = float(top)

    def bit_step(s, thr):
        cand = jnp.where(s == 0, 0, thr | lax.shift_left(jnp.int32(1), 31 - s))
        cnt = count(lambda k, idx: k >= cand)
        return jnp.where(cnt >= topf, cand, thr)

    thr = lax.fori_loop(0, 32, bit_step, jnp.full(shape, INT_MIN, I32))
    thr = jnp.maximum(thr, KEY_NEG_INF + 1)
    n_gt = count(lambda k, idx: k > thr)
    n_ge = count(lambda k, idx: k >= thr)
    return thr, n_ge, topf - n_gt


def _tie_bound(count, shape, thr, need, n_ge, top, idx_bits):
    full = jnp.full(shape, (1 << idx_bits) - 1, I32)

    def search():
        def bit_step(s, pos):
            cand = pos + lax.shift_left(jnp.int32(1), idx_bits - 1 - s)
            cnt = count(lambda k, idx: (k == thr) & (idx < cand))
            return jnp.where(cnt < need, cand, pos)
        return lax.fori_loop(0, idx_bits, bit_step, jnp.zeros(shape, I32))

    excess = jnp.max(n_ge) > float(top)
    return lax.cond(excess, search, lambda: full)


DSA_BLOCK = 256
SEARCH_BLOCKS = 2
ATTEND_ORDER = ((0, 0), (1, 0), (0, 1), (2, 0), (1, 1), (2, 1))
ATTEND_UNROLL = 4


I16 = jnp.int16
I16_MIN = -(2 ** 15)
HI_NEG_INF = KEY_NEG_INF >> 16
LO_NEG_INF = ((KEY_NEG_INF & 0xFFFF) ^ 0x8000) - 0x10000 * (((KEY_NEG_INF & 0xFFFF) ^ 0x8000) >> 15)
PACKED_ROWS = 2 * SUBLANES


def _search16(count_ge, need, shape, total):
    def step(s, carry):
        t, n_at, n_next = carry
        cand = jnp.where(s == 0, 0, t | lax.shift_left(jnp.int32(1), 15 - s))
        cnt = count_ge(cand.astype(I16))
        ok = cnt >= need
        return jnp.where(ok, cand, t), jnp.where(ok, cnt, n_at), jnp.where(ok, n_next, cnt)
    init = (jnp.full(shape, I16_MIN, I32), jnp.broadcast_to(total, shape), jnp.zeros(shape, I32))
    return lax.fori_loop(0, 16, step, init)


def _dsa_prompt_kernel(qt_ref, q6t_ref, wt_ref, k6_ref, k_ref, vt_ref, bias_ref, o_ref,
                       mask_ref, hi_ref, lo_ref, qm_ref, m_ref, acc_ref, s_ref, p_ref, a_ref,
                       *, heads, hd, idx_heads, top, seq):
    blk = DSA_BLOCK
    i = pl.program_id(1)
    pair = LANES // hd
    n_search = (i + SEARCH_BLOCKS) // SEARCH_BLOCKS
    krow = lax.broadcasted_iota(I32, (blk, blk), 0)
    qpos = i * blk + lax.broadcasted_iota(I32, (blk, blk), 1)
    shape = (1, blk)

    def score_block(j, causal):
        kc = k6_ref[pl.ds(pl.multiple_of(j * blk, blk), blk), :]
        sc = jnp.zeros((blk, blk), F32)
        for h in range(idx_heads):
            sc = sc + wt_ref[0, h:h + 1, :] * jnp.maximum(_dot(kc, q6t_ref[0, h]), 0.0)
        if causal:
            sc = jnp.where(j * blk + krow <= qpos, sc, -jnp.inf)
        key = _float_key(sc)
        hi_ref[j] = lax.shift_right_arithmetic(key, 16).astype(I16)
        lo_ref[j] = (key ^ 0x8000).astype(I16)

    def score_blocks(c, _):
        for b in range(SEARCH_BLOCKS):
            score_block(c * SEARCH_BLOCKS + b, False)
        return 0

    lax.fori_loop(0, n_search - 1, score_blocks, 0)
    for b in range(SEARCH_BLOCKS):
        score_block((n_search - 1) * SEARCH_BLOCKS + b, True)

    def idx16(j):
        return (j * blk + krow).astype(I16)

    def count(pred):
        def body(c, acc):
            for b in range(SEARCH_BLOCKS):
                j = c * SEARCH_BLOCKS + b
                ind = jnp.where(pred(hi_ref[j], lo_ref[j], j), jnp.ones((), I16), jnp.zeros((), I16))
                for r in range(blk // PACKED_ROWS):
                    acc = acc + ind[r * PACKED_ROWS:(r + 1) * PACKED_ROWS, :]
            return acc
        acc = lax.fori_loop(0, n_search, body, jnp.zeros((PACKED_ROWS, blk), I16))
        return jnp.sum(acc.astype(I32), axis=0, keepdims=True)

    total = n_search * (SEARCH_BLOCKS * blk)
    hi_t, n_hi_ge, n_above = _search16(lambda c: count(lambda hi, lo, j: hi >= c), top, shape, total)
    hi_t16 = hi_t.astype(I16)
    n_band = n_hi_ge - n_above

    def band_block(j, _):
        lo_ref[j] = jnp.where(hi_ref[j] == hi_t16, lo_ref[j], jnp.full((), I16_MIN, I16))
        return 0

    lax.fori_loop(0, n_search * SEARCH_BLOCKS, band_block, 0)
    lo_t, n_lo_ge, n_lo_gt = _search16(lambda c: count(lambda hi, lo, j: lo >= c), top - n_above, shape, total)
    lo_t16 = lo_t.astype(I16)
    n_gt = n_above + n_lo_gt
    n_ge = n_above + jnp.where(lo_t == I16_MIN, n_band, n_lo_ge)
    need = top - n_gt
    short = hi_t <= HI_NEG_INF
    hi_t16 = jnp.where(short, HI_NEG_INF, hi_t).astype(I16)
    lo_t16 = jnp.where(short, LO_NEG_INF, lo_t).astype(I16)
    n_ge = jnp.where(short, 0, n_ge)

    idx_bits = (seq - 1).bit_length()

    def tie_search():
        def step(s, pos):
            cand = pos + lax.shift_left(jnp.int32(1), idx_bits - 1 - s)
            c16 = cand.astype(I16)
            cnt = count(lambda hi, lo, j: (hi == hi_t16) & (lo == lo_t16) & (idx16(j) < c16))
            return jnp.where(cnt < need, cand, pos)
        return lax.fori_loop(0, idx_bits, step, jnp.zeros(shape, I32))

    pos = lax.cond(jnp.max(n_ge) > top, tie_search, lambda: jnp.full(shape, (1 << idx_bits) - 1, I32))
    pos16 = jnp.where(short, -1, pos).astype(I16)

    def mask_block(j, _):
        hi, lo = hi_ref[j], lo_ref[j]
        tie = (lo == lo_t16) & (idx16(j) <= pos16)
        sel = (hi > hi_t16) | ((hi == hi_t16) & ((lo > lo_t16) | tie))
        mask_ref[j] = jnp.where(sel, jnp.zeros((), BF16), jnp.full((), NEG_INF, BF16)).astype(F32)
        return 0

    lax.fori_loop(0, i + 1, mask_block, 0)

    sub = lax.broadcasted_iota(I32, (LANES, blk), 0)
    for h in range(heads):
        g = h // pair
        own = (sub >= (h % pair) * hd) & (sub < (h % pair + 1) * hd)
        qm_ref[h] = jnp.where(own, qt_ref[0, g * LANES:(g + 1) * LANES, :], jnp.zeros((), BF16))

    m_ref[...] = jnp.full(m_ref.shape, NEG_INF, F32)
    acc_ref[...] = jnp.zeros(acc_ref.shape, F32)

    def attend(j, near):
        rows = pl.ds(pl.multiple_of(j * blk, blk), blk)
        addm = mask_ref[j]
        def logits(h):
            g = h // pair
            s = _dot(k_ref[rows, g * LANES:(g + 1) * LANES], qm_ref[h]) + addm
            if near is not None:
                s = s + bias_ref[near, h]
            s_ref[h] = s
            m_old = m_ref[h]
            m_new = jnp.maximum(m_old, jnp.max(s, axis=0, keepdims=True))
            a_ref[h] = jnp.exp2(m_old - m_new)
            m_ref[h] = m_new

        def probs(h):
            p_ref[h] = jnp.exp2(s_ref[h] - m_ref[h]).astype(BF16)

        def values(h):
            acc_ref[h] = a_ref[h] * acc_ref[h] + _dot(vt_ref[0, j, h], p_ref[h])

        for stage, group in ATTEND_ORDER:
            for h in range(group * heads // 2, (group + 1) * heads // 2):
                (logits, probs, values)[stage](h)

    n_far = jnp.maximum(i - 1, 0)

    def far_group(c, _):
        for u in range(ATTEND_UNROLL):
            attend(ATTEND_UNROLL * c + u, None)
        return 0

    def far_single(j, _):
        attend(j, None)
        return 0

    n_group = n_far // ATTEND_UNROLL
    lax.fori_loop(0, n_group, far_group, 0)
    lax.fori_loop(n_group * ATTEND_UNROLL, n_far, far_single, 0)

    @pl.when(i >= 1)
    def _():
        attend(i - 1, 1)

    attend(i, 0)

    for g in range(heads // pair):
        o = jnp.concatenate([acc_ref[g * pair + r, 0:hd, :] / acc_ref[g * pair + r, hd:hd + 1, :]
                             for r in range(pair)], axis=0)
        o_ref[:, g * LANES:(g + 1) * LANES] = o.T


def _dsa_prompt(qt, q6t, wt, k6, k_bf, vt, bias, nb, seq, heads, top):
    blk = DSA_BLOCK
    w = k_bf.shape[1]
    hd = w // heads
    idx_heads, kd = q6t.shape[1], q6t.shape[2]
    vrows = vt.shape[3]
    assert seq % (blk * SEARCH_BLOCKS) == 0 and LANES % hd == 0 and blk >= REL_MAX_DIST
    assert seq <= 2 ** 15 and vrows > hd
    nq = seq // blk
    once = pl.Buffered(1)
    kern = functools.partial(_dsa_prompt_kernel, heads=heads, hd=hd, idx_heads=idx_heads, top=top, seq=seq)
    return pl.pallas_call(
        kern,
        grid=(nb, nq),
        in_specs=[
            pl.BlockSpec((1, w, blk), lambda n, i: (n, 0, i)),
            pl.BlockSpec((1, idx_heads, kd, blk), lambda n, i: (n, 0, 0, i)),
            pl.BlockSpec((1, idx_heads, blk), lambda n, i: (n, 0, i)),
            pl.BlockSpec((seq, kd), lambda n, i: (n, 0), pipeline_mode=once),
            pl.BlockSpec((seq, w), lambda n, i: (n, 0), pipeline_mode=once),
            pl.BlockSpec((1, nq, heads, vrows, blk), lambda n, i: (n, 0, 0, 0, 0), pipeline_mode=once),
            pl.BlockSpec((2, heads, blk, blk), lambda n, i: (0, 0, 0, 0), pipeline_mode=once),
        ],
        out_specs=pl.BlockSpec((blk, w), lambda n, i: (n * nq + i, 0)),
        out_shape=jax.ShapeDtypeStruct((nb * seq, w), F32),
        scratch_shapes=[pltpu.VMEM((nq, blk, blk), F32),
                        pltpu.VMEM((nq, blk, blk), I16),
                        pltpu.VMEM((nq, blk, blk), I16),
                        pltpu.VMEM((heads, LANES, blk), BF16),
                        pltpu.VMEM((heads, 1, blk), F32),
                        pltpu.VMEM((heads, vrows, blk), F32),
                        pltpu.VMEM((heads, blk, blk), F32),
                        pltpu.VMEM((heads, blk, blk), BF16),
                        pltpu.VMEM((heads, 1, blk), F32)],
        compiler_params=_params("parallel", "arbitrary"),
        name="dsa_prompt",
    )(qt, q6t, wt, k6, k_bf, vt, bias)


def _dsa_sample_scores_kernel(pt_ref, q_ref, w_ref, kown_ref, *rest, n_pages, page):
    del pt_ref
    pages, o_ref = rest[:n_pages], rest[n_pages]
    q = q_ref[0]
    qh, qm, ql = _split3(q)
    w = w_ref[0]
    for p in range(n_pages):
        kh, km, kl = _split3(pages[p][0])
        d = (_dot(qh, kh) + _dot(qh, km) + _dot(qm, kh)
             + _dot(qh, kl) + _dot(qm, km) + _dot(ql, kh))
        sc = jnp.sum(w * jnp.maximum(d, 0.0), axis=0, keepdims=True)
        o_ref[0, :, p * page:(p + 1) * page] = sc
    d_own = jnp.sum(q * kown_ref[0], axis=-1, keepdims=True)
    s_own = jnp.sum(w * jnp.maximum(d_own, 0.0), axis=0, keepdims=True)
    lane = lax.broadcasted_iota(I32, (1, LANES), 1)
    o_ref[0, :, n_pages * page:] = jnp.where(lane == 0, s_own, -jnp.inf)


def _dsa_sample_scores(page_table, q_idx, w_idx, k_own, kidx_t):
    nb, n_pages = page_table.shape
    idx_dim, page = kidx_t.shape[1:]
    assert page % LANES == 0
    width = n_pages * page + LANES
    kern = functools.partial(_dsa_sample_scores_kernel, n_pages=n_pages, page=page)
    page_specs = [pl.BlockSpec((1, idx_dim, page), lambda n, pt, p=p: (pt[n * n_pages + p], 0, 0))
                  for p in range(n_pages)]
    grid_spec = pltpu.PrefetchScalarGridSpec(
        num_scalar_prefetch=1,
        grid=(nb,),
        in_specs=[pl.BlockSpec((1,) + q_idx.shape[1:], lambda n, pt: (n, 0, 0)),
                  pl.BlockSpec((1,) + w_idx.shape[1:], lambda n, pt: (n, 0, 0)),
                  pl.BlockSpec((1, 1, idx_dim), lambda n, pt: (n, 0, 0))] + page_specs,
        out_specs=pl.BlockSpec((1, 1, width), lambda n, pt: (n, 0, 0)),
    )
    return pl.pallas_call(
        kern,
        grid_spec=grid_spec,
        out_shape=jax.ShapeDtypeStruct((nb, 1, width), F32),
        compiler_params=_params("arbitrary"),
        name="dsa_sample_scores",
    )(page_table.reshape(-1), q_idx, w_idx, k_own, *([kidx_t] * n_pages))


def _dsa_sample_mask_kernel(s_ref, o_ref, *, top, width):
    keys = _float_key(s_ref[...])
    idx = lax.broadcasted_iota(I32, keys.shape, 1)

    def count(pred):
        return jnp.sum(jnp.where(pred(keys, idx), 1.0, 0.0), axis=-1, keepdims=True)

    shape = (keys.shape[0], 1)
    thr, n_ge, need = _topk_select(count, shape, top)
    pos = _tie_bound(count, shape, thr, need, n_ge, top, (width - 1).bit_length())
    sel = (keys > thr) | ((keys == thr) & (idx <= pos))
    o_ref[...] = jnp.where(sel, 0.0, NEG_INF)


def _dsa_sample_mask(scores, top):
    rows, width = scores.shape
    return pl.pallas_call(
        functools.partial(_dsa_sample_mask_kernel, top=top, width=width),
        out_shape=jax.ShapeDtypeStruct((rows, width), F32),
        compiler_params=pltpu.CompilerParams(vmem_limit_bytes=VMEM_LIMIT),
        name="dsa_sample_mask",
    )(scores)


def _dsa_sample_attend_kernel(pt_ref, q_ref, kown_ref, vown_ref, mask_ref, bias_ref, *rest,
                              n_pages, page, heads, hd):
    del pt_ref
    kp, vp, o_ref = rest[:n_pages], rest[n_pages:2 * n_pages], rest[2 * n_pages]
    past = n_pages * page
    hm = _head_mask(heads, hd)
    q_bd = jnp.where(hm, jnp.broadcast_to(q_ref[0] * (hd ** -0.5), hm.shape), 0.0)
    q_bf = q_bd.astype(BF16)
    logits = []
    for p in range(n_pages):
        sl = slice(p * page, (p + 1) * page)
        logits.append(_dot(q_bf, kp[p][0].astype(BF16)) + bias_ref[:, sl] + mask_ref[0, :, sl])
    s_own = (jnp.sum(q_bd * kown_ref[0], axis=-1, keepdims=True)
             + bias_ref[:, past:past + 1] + mask_ref[0, :, past:past + 1])
    m = s_own
    for s in logits:
        m = jnp.maximum(m, jnp.max(s, axis=-1, keepdims=True))
    p_own = jnp.exp(s_own - m)
    l = p_own
    acc = p_own * vown_ref[0]
    for p in range(n_pages):
        pr = jnp.exp(logits[p] - m)
        l = l + jnp.sum(pr, axis=-1, keepdims=True)
        acc = acc + _dot_nt(pr.astype(BF16), vp[p][0].astype(BF16))
    o_ref[0] = jnp.sum(jnp.where(hm, acc / l, 0.0), axis=0, keepdims=True)


def _dsa_sample_attend(page_table, q, k_own, v_own, mask, bias, k_t, v_t, heads):
    nb, n_pages = page_table.shape
    w, page = k_t.shape[1:]
    assert heads == SUBLANES
    width = mask.shape[-1]
    kern = functools.partial(_dsa_sample_attend_kernel, n_pages=n_pages, page=page, heads=heads, hd=w // heads)
    page_specs = [pl.BlockSpec((1, w, page), lambda n, pt, p=p: (pt[n * n_pages + p], 0, 0))
                  for p in range(n_pages)]
    row = pl.BlockSpec((1, 1, w), lambda n, pt: (n, 0, 0))
    grid_spec = pltpu.PrefetchScalarGridSpec(
        num_scalar_prefetch=1,
        grid=(nb,),
        in_specs=[row, row, row,
                  pl.BlockSpec((1, 1, width), lambda n, pt: (n, 0, 0)),
                  pl.BlockSpec((heads, width), lambda n, pt: (0, 0))] + page_specs + page_specs,
        out_specs=pl.BlockSpec((1, 1, w), lambda n, pt: (n, 0, 0)),
    )
    return pl.pallas_call(
        kern,
        grid_spec=grid_spec,
        out_shape=jax.ShapeDtypeStruct((nb, 1, w), F32),
        compiler_params=_params("arbitrary"),
        name="dsa_sample_attend",
    )(page_table.reshape(-1), q, k_own, v_own, mask, bias, *([k_t] * n_pages), *([v_t] * n_pages))


def _merge_kernel(x_ref, y_ref, a_ref, m_ref, g1_ref, g2_ref, g3_ref, wc_ref, wa_ref, wm_ref, wo_ref,
                  lg_ref, lb_ref, o_ref):
    c = _dot(y_ref[...].astype(BF16), wc_ref[...])
    a = _dot(a_ref[...].astype(BF16), wa_ref[...])
    m = _dot(m_ref[...].astype(BF16), wm_ref[...])
    merged = (jax.nn.sigmoid(g1_ref[...]) * c + jax.nn.sigmoid(g2_ref[...]) * a
              + jax.nn.sigmoid(g3_ref[...]) * m)
    h = _dot(merged.astype(BF16), wo_ref[...])
    o_ref[...] = _layer_norm(ALPHA * x_ref[...] + h, lg_ref[...], lb_ref[...])


def _merge(x, y_act, att, mem, z_main, g_col, wc, wa, wm, wo, ln_g, ln_b):
    n, d = x.shape
    tm = _pick_tile(n, (256, 128, 64, 32, 16, 8))
    row = lambda width: pl.BlockSpec((tm, width), lambda i: (i, 0))
    gate = lambda b: pl.BlockSpec((tm, d), lambda i: (i, g_col + b))
    full = lambda a: pl.BlockSpec(a.shape, lambda i: (0, 0))
    return pl.pallas_call(
        _merge_kernel,
        grid=(n // tm,),
        in_specs=[row(d), row(y_act.shape[1]), row(att.shape[1]), row(mem.shape[1]),
                  gate(0), gate(1), gate(2), full(wc), full(wa), full(wm), full(wo), full(ln_g), full(ln_b)],
        out_specs=row(d),
        out_shape=jax.ShapeDtypeStruct((n, d), F32),
        compiler_params=_params("parallel"),
        name="merge_ln1",
    )(x, y_act, att, mem, z_main, z_main, z_main, wc, wa, wm, wo, ln_g, ln_b)


MOE_BLOCK = 256
MOE_TOKENS = 128
DMA_PRIORITIES = 2


U32 = jnp.uint32
HIGH_HALF = 0xFFFF0000


def _pack_halves(x):
    c = x.shape[1] // 2
    lo = lax.bitcast_convert_type(x[:, :c].astype(BF16).astype(F32), U32) >> 16
    hi = lax.bitcast_convert_type(x[:, c:].astype(BF16).astype(F32), U32) & jnp.uint32(HIGH_HALF)
    return lo | hi


def _unpack_halves(u):
    return (lax.bitcast_convert_type(u << 16, F32),
            lax.bitcast_convert_type(u & jnp.uint32(HIGH_HALF), F32))


def _router_kernel(x_ref, wh_ref, wl_ref, rb_ref, e_ref, g_ref, r_ref, cnt_ref, xp_ref, carry_ref, *, topk):
    i = pl.program_id(0)

    @pl.when(i == 0)
    def _():
        carry_ref[...] = jnp.zeros(carry_ref.shape, F32)

    xp_ref[...] = _pack_halves(x_ref[...])
    xh, xl = _split2(x_ref[...])
    logits = _dot(xh, wh_ref[...]) + _dot(xl, wh_ref[...]) + _dot(xh, wl_ref[...])
    scores = jax.nn.sigmoid(logits)
    tm, ne = scores.shape
    sel = scores + rb_ref[...]
    lane = lax.broadcasted_iota(I32, (tm, ne), 1).astype(F32)
    chosen = jnp.zeros((tm, ne), F32)
    picks = []
    total = jnp.zeros((tm, 1), F32)
    for _ in range(topk):
        mx = jnp.max(sel, axis=-1, keepdims=True)
        idx = jnp.min(jnp.where(sel == mx, lane, float(ne)), axis=-1, keepdims=True)
        hot = lane == idx
        sk = jnp.sum(jnp.where(hot, scores, 0.0), axis=-1, keepdims=True)
        sel = jnp.where(hot, -jnp.inf, sel)
        chosen = jnp.where(hot, 1.0, chosen)
        total = total + sk
        picks.append((idx, sk, hot))
    r = lax.broadcasted_iota(I32, (tm, tm), 0)
    c = lax.broadcasted_iota(I32, (tm, tm), 1)
    before = jnp.where(c < r, 1.0, 0.0).astype(BF16)
    rank = _dot(before, chosen.astype(BF16)) + carry_ref[...]
    out_lane = lax.broadcasted_iota(I32, (tm, LANES), 1)
    e_out = jnp.zeros((tm, LANES), I32)
    g_out = jnp.zeros((tm, LANES), F32)
    r_out = jnp.zeros((tm, LANES), I32)
    for k, (idx, sk, hot) in enumerate(picks):
        rk = jnp.sum(jnp.where(hot, rank, 0.0), axis=-1, keepdims=True)
        e_out = jnp.where(out_lane == k, idx.astype(I32), e_out)
        g_out = jnp.where(out_lane == k, sk / total * ROUTED_SCALE, g_out)
        r_out = jnp.where(out_lane == k, rk.astype(I32), r_out)
    e_ref[...] = e_out
    g_ref[...] = g_out
    r_ref[...] = r_out
    carry_ref[...] = carry_ref[...] + jnp.sum(chosen, axis=0, keepdims=True)
    cnt_ref[...] = carry_ref[...]


def _router(x1, w_router, router_bias):
    n, d = x1.shape
    ne = w_router.shape[1]
    tm = _pick_tile(n, (256, 128, 64, 48, 32, 16, 8))
    wh, wl = _split2(w_router)
    out = lambda: pl.BlockSpec((tm, LANES), lambda i: (i, 0))
    return pl.pallas_call(
        functools.partial(_router_kernel, topk=TOP_K),
        grid=(n // tm,),
        in_specs=[pl.BlockSpec((tm, d), lambda i: (i, 0)),
                  pl.BlockSpec((d, ne), lambda i: (0, 0)),
                  pl.BlockSpec((d, ne), lambda i: (0, 0)),
                  pl.BlockSpec((1, ne), lambda i: (0, 0))],
        out_specs=[out(), out(), out(), pl.BlockSpec((1, ne), lambda i: (0, 0)),
                   pl.BlockSpec((tm, d // 2), lambda i: (i, 0))],
        out_shape=[jax.ShapeDtypeStruct((n, LANES), I32), jax.ShapeDtypeStruct((n, LANES), F32),
                   jax.ShapeDtypeStruct((n, LANES), I32), jax.ShapeDtypeStruct((1, ne), F32),
                   jax.ShapeDtypeStruct((n, d // 2), U32)],
        scratch_shapes=[pltpu.VMEM((1, ne), F32)],
        compiler_params=_params("arbitrary"),
        name="router",
    )(x1, wh, wl, router_bias)


def _slots_kernel(e_ref, r_ref, ps_ref, o_ref, *, topk):
    e, r = e_ref[...], r_ref[...]
    tm, ne = e.shape[0], ps_ref.shape[1]
    lane = lax.broadcasted_iota(I32, (tm, ne), 1)
    out_lane = lax.broadcasted_iota(I32, e.shape, 1)
    out = jnp.zeros(e.shape, I32)
    for k in range(topk):
        start = jnp.sum(jnp.where(lane == e[:, k:k + 1], ps_ref[...], 0.0), axis=-1, keepdims=True)
        out = jnp.where(out_lane == k, start.astype(I32) + r, out)
    o_ref[...] = out


def _slots(top_e, rank, pstart):
    n = top_e.shape[0]
    tm = _pick_tile(n, (256, 128, 64, 48, 32, 16, 8))
    row = pl.BlockSpec((tm, LANES), lambda i: (i, 0))
    return pl.pallas_call(
        functools.partial(_slots_kernel, topk=TOP_K),
        grid=(n // tm,),
        in_specs=[row, row, pl.BlockSpec(pstart.shape, lambda i: (0, 0))],
        out_specs=row,
        out_shape=jax.ShapeDtypeStruct((n, LANES), I32),
        compiler_params=_params("parallel"),
        name="moe_slots",
    )(top_e, rank, pstart)


def _dispatch_kernel(dest_ref, x_ref, init_ref, xs_ref, sem, *, tt, topk):
    del init_ref

    def copy(t, d):
        return pltpu.make_async_copy(x_ref.at[pl.ds(t, 1)], xs_ref.at[pl.ds(d, 1)], sem)

    def start(t, _):
        for k in range(topk):
            copy(t, dest_ref[t * topk + k]).start(priority=k % DMA_PRIORITIES)
        return 0

    lax.fori_loop(0, tt, start, 0, unroll=4)

    def wait(t, _):
        for k in range(topk):
            copy(t, dest_ref[t * topk + k]).wait()
        return 0

    lax.fori_loop(0, tt, wait, 0)


def _dispatch(x1, dest, cap):
    n, d = x1.shape
    topk = dest.shape[1]
    tt = _pick_tile(n, (MOE_TOKENS, 64, 48, 32, 16, 8))
    init = jnp.zeros((cap, d), x1.dtype)
    return pl.pallas_call(
        functools.partial(_dispatch_kernel, tt=tt, topk=topk),
        grid=(n // tt,),
        in_specs=[pl.BlockSpec((tt * topk,), lambda i: (i,), memory_space=pltpu.SMEM),
                  pl.BlockSpec((tt, d), lambda i: (i, 0)),
                  pl.BlockSpec(memory_space=pl.ANY)],
        out_specs=pl.BlockSpec(memory_space=pl.ANY),
        out_shape=jax.ShapeDtypeStruct((cap, d), x1.dtype),
        scratch_shapes=[pltpu.SemaphoreType.DMA(())],
        input_output_aliases={2: 0},
        compiler_params=_params("arbitrary"),
        name="moe_dispatch",
    )(dest.reshape(-1), x1, init)


def _expert_kernel(be_ref, na_ref, xs_ref, wgu_ref, wdn_ref, y_ref, wgu_bf, wdn_bf, *, de):
    b = pl.program_id(0)

    @pl.when(b < na_ref[0])
    def _():
        @pl.when((b == 0) | (be_ref[b] != be_ref[jnp.maximum(b - 1, 0)]))
        def _():
            wgu_bf[...] = wgu_ref[0].astype(BF16)
            wdn_bf[...] = wdn_ref[0].astype(BF16)

        lo, hi = _unpack_halves(xs_ref[...])
        half = lo.shape[1]
        h = _dot(lo.astype(BF16), wgu_bf[0:half, :]) + _dot(hi.astype(BF16), wgu_bf[half:, :])
        act = _silu(h[:, :de]) * h[:, de:]
        y_ref[...] = _pack_halves(_dot(act.astype(BF16), wdn_bf[...]))

    @pl.when(b >= na_ref[0])
    def _():
        y_ref[...] = jnp.zeros(y_ref.shape, U32)


def _experts(xs, blk_exp, n_active, w_gu, w_down):
    cap, dp = xs.shape
    ne, d, de2 = w_gu.shape
    de = de2 // 2
    n_blk = cap // MOE_BLOCK
    grid_spec = pltpu.PrefetchScalarGridSpec(
        num_scalar_prefetch=2,
        grid=(n_blk,),
        in_specs=[pl.BlockSpec((MOE_BLOCK, dp), lambda b, be, na: (b, 0)),
                  pl.BlockSpec((1, d, de2), lambda b, be, na: (be[b], 0, 0)),
                  pl.BlockSpec((1, de, d), lambda b, be, na: (be[b], 0, 0))],
        out_specs=pl.BlockSpec((MOE_BLOCK, dp), lambda b, be, na: (b, 0)),
        scratch_shapes=[pltpu.VMEM((d, de2), BF16), pltpu.VMEM((de, d), BF16)],
    )
    return pl.pallas_call(
        functools.partial(_expert_kernel, de=de),
        grid_spec=grid_spec,
        out_shape=jax.ShapeDtypeStruct((cap, dp), U32),
        compiler_params=_params("arbitrary"),
        name="moe_experts",
    )(blk_exp, n_active, xs, w_gu, w_down)


def _combine_kernel(dest_ref, gate_ref, x_ref, y_ref, wgu_ref, wdn_ref, lg_ref, lb_ref, o_ref, buf, sem,
                    *, tt, topk, ds):
    def copy(t, k):
        return pltpu.make_async_copy(y_ref.at[pl.ds(dest_ref[t * topk + k], 1)], buf.at[k, pl.ds(t, 1)], sem)

    def start(t, _):
        for k in range(topk):
            copy(t, k).start(priority=k % DMA_PRIORITIES)
        return 0

    lax.fori_loop(0, tt, start, 0, unroll=4)

    x = x_ref[...]
    h = _dot(x.astype(BF16), wgu_ref[...])
    shared = _dot((_silu(h[:, :ds]) * h[:, ds:]).astype(BF16), wdn_ref[...])

    def wait(t, _):
        for k in range(topk):
            copy(t, k).wait()
        return 0

    lax.fori_loop(0, tt, wait, 0)

    r_lo = jnp.zeros(buf.shape[1:], F32)
    r_hi = jnp.zeros(buf.shape[1:], F32)
    for k in range(topk):
        lo, hi = _unpack_halves(buf[k])
        r_lo = r_lo + gate_ref[:, k:k + 1] * lo
        r_hi = r_hi + gate_ref[:, k:k + 1] * hi
    routed = jnp.concatenate([r_lo, r_hi], axis=-1)
    o_ref[...] = _layer_norm(ALPHA * x + routed + shared, lg_ref[...], lb_ref[...])


def _combine(x1, y_rows, dest, gates, w_sh_gu, w_sh_down, ln_g, ln_b):
    n, d = x1.shape
    topk = dest.shape[1]
    tt = _pick_tile(n, (MOE_TOKENS, 64, 48, 32, 16, 8))
    full = lambda a: pl.BlockSpec(a.shape, lambda i: (0, 0))
    return pl.pallas_call(
        functools.partial(_combine_kernel, tt=tt, topk=topk, ds=w_sh_down.shape[0]),
        grid=(n // tt,),
        in_specs=[pl.BlockSpec((tt * topk,), lambda i: (i,), memory_space=pltpu.SMEM),
                  pl.BlockSpec((tt, LANES), lambda i: (i, 0)),
                  pl.BlockSpec((tt, d), lambda i: (i, 0)),
                  pl.BlockSpec(memory_space=pl.ANY),
                  full(w_sh_gu), full(w_sh_down), full(ln_g), full(ln_b)],
        out_specs=pl.BlockSpec((tt, d), lambda i: (i, 0)),
        out_shape=jax.ShapeDtypeStruct((n, d), F32),
        scratch_shapes=[pltpu.VMEM((topk, tt, y_rows.shape[1]), U32), pltpu.SemaphoreType.DMA(())],
        compiler_params=_params("arbitrary"),
        name="moe_combine",
    )(dest.reshape(-1), gates, x1, y_rows, w_sh_gu, w_sh_down, ln_g, ln_b)


def _moe_ln2(x1, w_router, router_bias, w_exp_gu, w_exp_down, w_sh_gu, w_sh_down, ln_g, ln_b):
    n, d = x1.shape
    ne = w_router.shape[1]
    top_e, gates, rank, counts, x1_packed = _router(x1, w_router, router_bias)
    counts = counts[0].astype(I32)
    padded = (counts + MOE_BLOCK - 1) // MOE_BLOCK * MOE_BLOCK
    pend = jnp.cumsum(padded)
    pstart = pend - padded
    dest = _slots(top_e, rank, pstart.astype(F32)[None, :])[:, :TOP_K]
    cap = (n * TOP_K + ne * (MOE_BLOCK - 1) + MOE_BLOCK - 1) // MOE_BLOCK * MOE_BLOCK
    n_blk = cap // MOE_BLOCK
    blk_start = jnp.arange(n_blk, dtype=I32) * MOE_BLOCK
    blk_exp = jnp.minimum(jnp.sum((pend[None, :] <= blk_start[:, None]).astype(I32), axis=1), ne - 1)
    n_active = (pend[-1:] // MOE_BLOCK).astype(I32)
    xs = _dispatch(x1_packed, dest, cap)
    y_rows = _experts(xs, blk_exp, n_active, w_exp_gu, w_exp_down)
    return _combine(x1, y_rows, dest, gates, w_sh_gu.astype(BF16), w_sh_down.astype(BF16), ln_g, ln_b)


def _rel_bucket(dist):
    n = jnp.maximum(dist, 0)
    max_exact = N_BUCKETS // 2
    nf = jnp.maximum(n, max_exact).astype(F32)
    large = max_exact + (jnp.log(nf / max_exact) / math.log(REL_MAX_DIST / max_exact)
                         * (N_BUCKETS - max_exact)).astype(I32)
    return jnp.where(n < max_exact, n, jnp.minimum(large, N_BUCKETS - 1))


def _pages_t(cache):
    n_pool, page = cache.shape[:2]
    return jnp.moveaxis(cache, 1, -1).reshape(n_pool, -1, page)


def _bias_table(rel_bias, dist):
    hit = _rel_bucket(dist)[None, :, None] == jnp.arange(N_BUCKETS, dtype=I32)[None, None, :]
    return jnp.sum(jnp.where(hit, rel_bias.T[:, None, :], 0.0), axis=-1)


def kernel(x_prompt, x_sample, mem_prompt, cache_k, cache_v, cache_kidx, cache_mem_k, cache_mem_v, state_conv, page_table, rel_bias, w_in, b_in, conv_w, conv_b, conv_ln_g, conv_ln_b, w_conv_out, w_attn_out, w_mem_kv, w_mem_out, w_out, ln1_g, ln1_b, w_router, router_bias, w_exp_gu, w_exp_down, w_sh_gu, w_sh_down, ln2_g, ln2_b):
    assert w_in.shape[0] == DEPTH == 1
    nb, seq, d = x_prompt.shape
    nsb, nst, _ = x_sample.shape
    assert nst == 1
    width, dc = conv_w.shape[1:]
    _, n_pool, page, heads, hd = cache_k.shape
    idx_dim = cache_kidx.shape[-1]
    n_mem, mem_heads, mem_hd = cache_mem_k.shape[2:]
    aw, mw, iw = heads * hd, mem_heads * mem_hd, N_IDX_HEADS * idx_dim
    assert 2 * dc == d and aw * 2 == d and mw * 2 == d

    o_q = 2 * dc
    o_qi = o_q + 3 * aw
    o_ki = o_qi + iw
    o_wi = o_ki + idx_dim
    o_qm = o_wi + N_IDX_HEADS
    o_g = o_qm + mw
    w0, b0 = w_in[0], b_in[0]
    w_main = jnp.concatenate([w0[:, :o_qi], w0[:, o_qm:]], axis=1).astype(BF16)
    b_main = jnp.concatenate([b0[:o_qi], b0[o_qm:]])[None, :]
    n_idx = o_qm - o_qi
    n_idx_pad = -(-n_idx // LANES) * LANES
    w_idx = jnp.pad(w0[:, o_qi:o_qm], ((0, 0), (0, n_idx_pad - n_idx)))
    b_idx = jnp.pad(b0[o_qi:o_qm], (0, n_idx_pad - n_idx))[None, :]
    col_q, col_k, col_v, col_qm, col_g = 2, 3, 4, 5, 3

    xp = x_prompt.reshape(nb * seq, d)
    xs = x_sample.reshape(nsb, d)
    zp = _linear(xp, w_main, b_main, tn=1024)
    zs = _linear(xs, w_main, b_main, tn=1024)
    zip_ = _linear(xp, w_idx, b_idx, split=True, tn=n_idx_pad)
    zis = _linear(xs, w_idx, b_idx, split=True, tn=n_idx_pad)

    kp, vp = zp[:, col_k * aw:(col_k + 1) * aw], zp[:, col_v * aw:(col_v + 1) * aw]
    ks, vs = zs[:, col_k * aw:(col_k + 1) * aw], zs[:, col_v * aw:(col_v + 1) * aw]
    kip, kis = zip_[:, iw:iw + idx_dim], zis[:, iw:iw + idx_dim]

    conv0 = jnp.zeros((nb, width - 1, dc), F32)
    cw, cb, cg, cbb = conv_w[0], conv_b[0][None, :], conv_ln_g[0][None, :], conv_ln_b[0][None, :]
    yp, tail_p = _conv_prompt(zp, conv0, cw, cb, cg, cbb, nb, seq)
    conv_p = tail_p[:, CONV_HALO - (width - 1):, :]
    ys, new_t = _conv_sample(zs, jnp.swapaxes(state_conv[0], 0, 1), cw, cb, cg, cbb)
    conv_s = jnp.swapaxes(new_t, 0, 1)

    wkv = w_mem_kv[0].astype(BF16)
    mkv = _linear(mem_prompt.reshape(nb * n_mem, d), wkv, jnp.zeros((1, 2 * mw), F32))
    mk_p, mv_p = mkv[:, :mw], mkv[:, mw:]
    mem_p = _mem_prompt(zp, col_qm, mk_p, mv_p, nb, seq, mem_heads)
    mem_s = _mem_sample(zs, col_qm, cache_mem_k[0].reshape(nsb, n_mem, mw),
                        cache_mem_v[0].reshape(nsb, n_mem, mw), mem_heads)

    far = _bias_table(rel_bias, jnp.full((1,), 4 * REL_MAX_DIST, I32))
    a = jnp.arange(DSA_BLOCK, dtype=I32)
    dist = jnp.stack([a[None, :] - a[:, None], DSA_BLOCK + a[None, :] - a[:, None]])
    bias_p = (_bias_table(rel_bias, dist.reshape(-1)) - far).reshape(heads, 2, DSA_BLOCK, DSA_BLOCK)
    bias_p = jnp.swapaxes(bias_p, 0, 1) * LOG2E
    qh, qm_, ql = _split3(zip_[:, :iw].reshape(nb, seq, N_IDX_HEADS, idx_dim))
    q6t = jnp.transpose(jnp.concatenate([qh, qh, qm_, qh, qm_, ql], axis=-1), (0, 2, 3, 1))
    kh, km, kl = _split3(kip)
    k6 = jnp.concatenate([kh, km, kh, kl, km, kh], axis=-1)
    wt = jnp.swapaxes(zip_[:, iw + idx_dim:iw + idx_dim + N_IDX_HEADS].reshape(nb, seq, N_IDX_HEADS), 1, 2)
    qt = jnp.swapaxes((zp[:, col_q * aw:(col_q + 1) * aw] * (hd ** -0.5 * LOG2E)).astype(BF16)
                      .reshape(nb, seq, aw), 1, 2)
    nkb = seq // DSA_BLOCK
    vt = jnp.transpose(vp.astype(BF16).reshape(nb, nkb, DSA_BLOCK, heads, hd), (0, 1, 3, 4, 2))
    vt = jnp.concatenate([vt, jnp.ones((nb, nkb, heads, 1, DSA_BLOCK), BF16),
                          jnp.zeros((nb, nkb, heads, PACKED_ROWS - 1, DSA_BLOCK), BF16)], axis=3)
    top_p = min(TOPK_MAX, seq // 4)
    att_p = _dsa_prompt(qt, q6t, wt, k6, kp.astype(BF16), vt, bias_p, nb, seq, heads, top_p)

    n_pages = page_table.shape[1]
    past = n_pages * page
    top_s = min(TOPK_MAX, (past + 1) // 4)
    scores = _dsa_sample_scores(page_table, zis[:, :iw].reshape(nsb, N_IDX_HEADS, idx_dim),
                                zis[:, iw + idx_dim:iw + idx_dim + N_IDX_HEADS].reshape(nsb, N_IDX_HEADS, 1),
                                kis.reshape(nsb, 1, idx_dim), _pages_t(cache_kidx[0]))
    mask_s = _dsa_sample_mask(scores.reshape(nsb, past + LANES), top_s)
    spos = jnp.arange(past + LANES, dtype=I32)
    bias_s = _bias_table(rel_bias, past - spos)
    att_s = _dsa_sample_attend(page_table, zs[:, col_q * aw:(col_q + 1) * aw].reshape(nsb, 1, aw),
                               ks.reshape(nsb, 1, aw), vs.reshape(nsb, 1, aw),
                               mask_s.reshape(nsb, 1, past + LANES), bias_s,
                               _pages_t(cache_k[0]), _pages_t(cache_v[0]), heads).reshape(nsb, aw)

    wc, wa, wm, wo = (w_conv_out[0].astype(BF16), w_attn_out[0].astype(BF16),
                      w_mem_out[0].astype(BF16), w_out[0].astype(BF16))
    l1g, l1b = ln1_g[0][None, :], ln1_b[0][None, :]
    x1p = _merge(xp, yp, att_p, mem_p, zp, col_g, wc, wa, wm, wo, l1g, l1b)
    x1s = _merge(xs, ys, att_s, mem_s, zs, col_g, wc, wa, wm, wo, l1g, l1b)
    x1 = jnp.concatenate([x1p, x1s], axis=0)
    x2 = _moe_ln2(x1, w_router[0], router_bias[0][None, :], w_exp_gu[0], w_exp_down[0],
                  w_sh_gu[0], w_sh_down[0], ln2_g[0][None, :], ln2_b[0][None, :])

    return (x2[:nb * seq].reshape(nb, seq, d), x2[nb * seq:].reshape(nsb, 1, d),
            kp.reshape(1, nb, seq, heads, hd), vp.reshape(1, nb, seq, heads, hd),
            kip.reshape(1, nb, seq, idx_dim),
            mk_p.reshape(1, nb, n_mem, mem_heads, mem_hd), mv_p.reshape(1, nb, n_mem, mem_heads, mem_hd),
            conv_p[None],
            ks.reshape(1, nsb, 1, heads, hd), vs.reshape(1, nsb, 1, heads, hd),
            kis.reshape(1, nsb, 1, idx_dim), conv_s[None])
```
